```python
import math
import jax, jax.numpy as jnp
from jax import lax
import numpy as np

D_MODEL = 4096
BATCH = 2
SEQ = 4096
DEPTH = 4

N_BRANCH = 3
BRANCH_WIDTH = D_MODEL // 4
DA_HEADS = 8
DA_SUB = BRANCH_WIDTH // (2 * DA_HEADS)
DA_HEAD = 2 * DA_SUB
DA_ROT = DA_SUB // 4
LRU_WIDTH = BRANCH_WIDTH
LRU_BLOCKS = 8
LRU_BW = LRU_WIDTH // LRU_BLOCKS
CONV_W = 4
LRU_C = 8.0
MLA_HEADS = 8
MLA_NOPE = 128
MLA_ROPE = 64
MLA_V = BRANCH_WIDTH // MLA_HEADS
MLA_QK = MLA_NOPE + MLA_ROPE
Q_LORA = 3 * D_MODEL // 16
KV_LORA = D_MODEL // 8
GATE_RANK = D_MODEL // 16
FFN_HIDDEN = -(-(8 * D_MODEL) // (3 * 256)) * 256
ROPE_THETA = 500000.0
Q_BLOCK = 128
EPS = 1e-6
IN_SIZES = (DA_HEADS * DA_HEAD, DA_HEADS * DA_HEAD, DA_HEADS * DA_HEAD,
            LRU_WIDTH, LRU_WIDTH, Q_LORA, KV_LORA, MLA_ROPE, GATE_RANK)
IN_COLS = 3 * DA_HEADS * DA_HEAD + 2 * LRU_WIDTH + Q_LORA + KV_LORA + MLA_ROPE + GATE_RANK

kernel_name = 'hybrid_diffattn_rglru_mla_encoder'


def _rms_norm(x, g, eps=EPS):
    xf = x.astype(jnp.float32)
    y = xf * lax.rsqrt(jnp.mean(xf * xf, axis=-1, keepdims=True) + eps)
    return (y * g.astype(jnp.float32)).astype(x.dtype)


def _rope_tables(positions, rot_dim):
    inv = ROPE_THETA ** (-jnp.arange(0, rot_dim, 2, dtype=jnp.float32) / rot_dim)
    ang = positions.astype(jnp.float32)[..., None] * inv
    return jnp.cos(ang), jnp.sin(ang)


def _apply_rope(x, cos, sin):
    half = cos.shape[-1]
    extra = x.ndim - 3
    shp = cos.shape[:2] + (1,) * extra + (half,)
    c = cos.reshape(shp).astype(x.dtype)
    s = sin.reshape(shp).astype(x.dtype)
    x1 = x[..., :half]
    x2 = x[..., half:2 * half]
    return jnp.concatenate([x1 * c - x2 * s, x2 * c + x1 * s, x[..., 2 * half:]], axis=-1)


def _over_query_blocks(fn, q):
    b, h, s = q.shape[:3]
    nb = s // Q_BLOCK
    qb = jnp.moveaxis(q.reshape((b, h, nb, Q_BLOCK) + q.shape[3:]), 2, 0)
    out = lax.map(fn, qb)
    out = jnp.moveaxis(out, 0, 2)
    return out.reshape(b, h, s, out.shape[-1])


def _split_last(h, sizes):
    outs = []
    start = 0
    for n in sizes:
        outs.append(h[..., start:start + n])
        start += n
    return outs


def _diff_attention(xq, xk, xv, cos, sin, q_g, k_g, lam_p, subln_g, lam_init):
    b, s = xq.shape[:2]
    q = xq.reshape(b, s, DA_HEADS, 2, DA_SUB)
    k = xk.reshape(b, s, DA_HEADS, 2, DA_SUB)
    v = xv.reshape(b, s, DA_HEADS, DA_HEAD).transpose(0, 2, 1, 3)
    q = _apply_rope(_rms_norm(q, q_g), cos, sin).transpose(0, 2, 1, 3, 4)
    k = _apply_rope(_rms_norm(k, k_g), cos, sin).transpose(0, 2, 1, 3, 4)
    lp = lam_p.astype(jnp.float32)
    lam = jnp.exp(jnp.sum(lp[0] * lp[1])) - jnp.exp(jnp.sum(lp[2] * lp[3])) + lam_init
    scale = DA_SUB ** -0.5

    def blk(qb):
        sc = jnp.einsum('bhqnd,bhknd->bhnqk', qb, k).astype(jnp.float32) * scale
        p = jax.nn.softmax(sc, axis=-1)
        pd = p[:, :, 0] - lam * p[:, :, 1]
        return jnp.einsum('bhqk,bhkd->bhqd', pd.astype(v.dtype), v)

    o = _over_query_blocks(blk, q)
    o = _rms_norm(o, subln_g) * (1.0 - lam_init)
    return o.transpose(0, 2, 1, 3).reshape(b, s, DA_HEADS * DA_HEAD)


def _linear_scan(a, bx, reverse):
    def comb(left, right):
        a1, b1 = left
        a2, b2 = right
        return a1 * a2, a2 * b1 + b2
    _, h = lax.associative_scan(comb, (a, bx), axis=1, reverse=reverse)
    return h


def _rglru_bidir(u, conv_w, conv_b, gate_w, gate_b, lam_L):
    b, s = u.shape[:2]
    pad_l = CONV_W // 2
    uc = lax.conv_general_dilated(
        u, conv_w[:, None, :].astype(u.dtype), window_strides=(1,),
        padding=[(pad_l, CONV_W - 1 - pad_l)],
        dimension_numbers=('NWC', 'WIO', 'NWC'),
        feature_group_count=LRU_WIDTH) + conv_b
    ub = uc.reshape(b, s, LRU_BLOCKS, LRU_BW)
    g = jnp.einsum('bsnc,zgncd->zgbsnd', ub, gate_w).reshape(2, 2, b, s, LRU_WIDTH)
    g = jax.nn.sigmoid((g + gate_b[:, :, None, None, :]).astype(jnp.float32))
    r, i = g[:, 0], g[:, 1]
    log_a = -LRU_C * r * jax.nn.softplus(-lam_L.astype(jnp.float32))[:, None, None, :]
    a = jnp.exp(log_a)
    mult = jnp.sqrt(-jnp.expm1(2.0 * log_a))
    bx = mult * i * uc.astype(jnp.float32)[None]
    h_f = _linear_scan(a[0], bx[0], reverse=False)
    h_b = _linear_scan(a[1], bx[1], reverse=True)
    return (h_f + h_b).astype(u.dtype)


def _mla(c_q, c_kv, k_pe, cos, sin, q_a_g, w_uq, kv_a_g, w_ukv, q_g, k_g):
    b, s = c_q.shape[:2]
    q = (_rms_norm(c_q, q_a_g) @ w_uq).reshape(b, s, MLA_HEADS, MLA_QK)
    kv = (_rms_norm(c_kv, kv_a_g) @ w_ukv).reshape(b, s, MLA_HEADS, MLA_NOPE + MLA_V)
    k_nope, v = kv[..., :MLA_NOPE], kv[..., MLA_NOPE:]
    k_pe_h = jnp.broadcast_to(k_pe[:, :, None, :], (b, s, MLA_HEADS, MLA_ROPE))
    k = jnp.concatenate([k_pe_h, k_nope], axis=-1)
    q = _apply_rope(_rms_norm(q, q_g), cos, sin).transpose(0, 2, 1, 3)
    k = _apply_rope(_rms_norm(k, k_g), cos, sin).transpose(0, 2, 1, 3)
    v = v.transpose(0, 2, 1, 3)
    scale = MLA_QK ** -0.5

    def blk(qb):
        sc = jnp.einsum('bhqd,bhkd->bhqk', qb, k).astype(jnp.float32) * scale
        p = jax.nn.softmax(sc, axis=-1)
        return jnp.einsum('bhqk,bhkd->bhqd', p.astype(v.dtype), v)

    o = _over_query_blocks(blk, q)
    return o.transpose(0, 2, 1, 3).reshape(b, s, MLA_HEADS * MLA_V)


def setup_inputs(seed: int = 0) -> dict:
    key = jax.random.key(seed)
    ks = jax.random.split(key, 32)

    def nrm(k, shape, fan_in):
        return jax.random.normal(k, shape, jnp.float32) * (fan_in ** -0.5)

    def gain(k, shape):
        return 1.0 + 0.02 * jax.random.normal(k, shape, jnp.float32)

    def bias(k, shape, sc=0.02):
        return sc * jax.random.normal(k, shape, jnp.float32)

    L = DEPTH
    x = jax.random.normal(ks[0], (BATCH, SEQ, D_MODEL), jnp.float32)
    positions = (jnp.arange(SEQ, dtype=jnp.int32)[None, :]
                 + jax.random.randint(ks[1], (BATCH, 1), 0, 1024, dtype=jnp.int32))
    a8 = jax.random.uniform(ks[12], (L, 2, LRU_WIDTH), jnp.float32, 0.9, 0.999)
    a_base = a8 ** (1.0 / LRU_C)
    lru_L = jnp.log(a_base) - jnp.log1p(-a_base)
    return {
        'x': x,
        'positions': positions,
        'mixer_norm_g': gain(ks[2], (L, D_MODEL)),
        'w_in': nrm(ks[3], (L, D_MODEL, IN_COLS), D_MODEL),
        'da_q_norm_g': gain(ks[4], (L, DA_SUB)),
        'da_k_norm_g': gain(ks[5], (L, DA_SUB)),
        'da_lambda': 0.1 * jax.random.normal(ks[6], (L, 4, DA_SUB), jnp.float32),
        'da_subln_g': gain(ks[7], (L, DA_HEAD)),
        'lru_conv_w': nrm(ks[8], (L, CONV_W, LRU_WIDTH), CONV_W),
        'lru_conv_b': bias(ks[9], (L, LRU_WIDTH)),
        'lru_gate_w': nrm(ks[10], (L, 2, 2, LRU_BLOCKS, LRU_BW, LRU_BW), LRU_BW),
        'lru_gate_b': bias(ks[11], (L, 2, 2, LRU_WIDTH)),
        'lru_L': lru_L,
        'mla_q_a_norm_g': gain(ks[13], (L, Q_LORA)),
        'mla_w_uq': nrm(ks[14], (L, Q_LORA, MLA_HEADS * MLA_QK), Q_LORA),
        'mla_kv_a_norm_g': gain(ks[15], (L, KV_LORA)),
        'mla_w_ukv': nrm(ks[16], (L, KV_LORA, MLA_HEADS * (MLA_NOPE + MLA_V)), KV_LORA),
        'mla_q_norm_g': gain(ks[17], (L, MLA_QK)),
        'mla_k_norm_g': gain(ks[18], (L, MLA_QK)),
        'w_gate_up': nrm(ks[19], (L, GATE_RANK, N_BRANCH * D_MODEL), GATE_RANK),
        'b_gate': bias(ks[20], (L, N_BRANCH, D_MODEL)),
        'w_branch_out': nrm(ks[21], (L, N_BRANCH, BRANCH_WIDTH, D_MODEL), BRANCH_WIDTH),
        'w_out': nrm(ks[22], (L, D_MODEL, D_MODEL), D_MODEL),
        'ffn_norm_g': gain(ks[23], (L, D_MODEL)),
        'w_ffn_gate': nrm(ks[24], (L, D_MODEL, FFN_HIDDEN), D_MODEL),
        'w_ffn_up': nrm(ks[25], (L, D_MODEL, FFN_HIDDEN), D_MODEL),
        'w_ffn_down': nrm(ks[26], (L, FFN_HIDDEN, D_MODEL), FFN_HIDDEN),
    }


def reference(x, positions, mixer_norm_g, w_in, da_q_norm_g, da_k_norm_g, da_lambda, da_subln_g,
              lru_conv_w, lru_conv_b, lru_gate_w, lru_gate_b, lru_L,
              mla_q_a_norm_g, mla_w_uq, mla_kv_a_norm_g, mla_w_ukv, mla_q_norm_g, mla_k_norm_g,
              w_gate_up, b_gate, w_branch_out, w_out,
              ffn_norm_g, w_ffn_gate, w_ffn_up, w_ffn_down):
    b, s = x.shape[:2]
    cos_a, sin_a = _rope_tables(positions, DA_ROT)
    cos_c, sin_c = _rope_tables(positions, MLA_ROPE)
    for l in range(DEPTH):
        lam_init = 0.8 - 0.6 * math.exp(-0.3 * l)
        xn = _rms_norm(x, mixer_norm_g[l])
        h = xn @ w_in[l]
        a_q, a_k, a_v, b_gate_in, b_x, c_qa, c_kva, c_kpe, g_lr = _split_last(h, IN_SIZES)
        o_a = _diff_attention(a_q, a_k, a_v, cos_a, sin_a, da_q_norm_g[l], da_k_norm_g[l],
                              da_lambda[l], da_subln_g[l], lam_init)
        o_b = _rglru_bidir(b_x, lru_conv_w[l], lru_conv_b[l], lru_gate_w[l], lru_gate_b[l],
                           lru_L[l]) * jax.nn.gelu(b_gate_in)
        o_c = _mla(c_qa, c_kva, c_kpe, cos_c, sin_c, mla_q_a_norm_g[l], mla_w_uq[l],
                   mla_kv_a_norm_g[l], mla_w_ukv[l], mla_q_norm_g[l], mla_k_norm_g[l])
        gates = jax.nn.sigmoid((g_lr @ w_gate_up[l]).reshape(b, s, N_BRANCH, D_MODEL) + b_gate[l])
        branches = jnp.stack([o_a, o_b, o_c], axis=0)
        y_br = jnp.einsum('nbsc,ncd->bsnd', branches, w_branch_out[l])
        x = x + jnp.sum(gates * y_br, axis=2) @ w_out[l]
        xn2 = _rms_norm(x, ffn_norm_g[l])
        x = x + (jax.nn.silu(xn2 @ w_ffn_gate[l]) * (xn2 @ w_ffn_up[l])) @ w_ffn_down[l]
    return x
```

```python
import functools
import math

import jax
import jax.numpy as jnp
from jax import lax
from jax.experimental import pallas as pl
from jax.experimental.pallas import tpu as pltpu

F32 = jnp.float32
BF16 = jnp.bfloat16

LANES = 128
SUBLANES = 8
VMEM_LIMIT_V7X = 60000 * 1024

EPS = 1e-6
ROPE_THETA = 500000.0
LRU_C = 8.0
CONV_W = 4
N_BRANCH = 3
DA_HEADS = 8
MLA_HEADS = 8
LRU_BLOCKS = 8


def _dims(d_model):
    bw = d_model // 4
    dm = dict(
        bw=bw,
        da_sub=bw // (2 * DA_HEADS),
        da_head=bw // DA_HEADS,
        lru_bw=bw // LRU_BLOCKS,
        nope=128,
        rope=64,
        mla_v=bw // MLA_HEADS,
        q_lora=3 * d_model // 16,
        kv_lora=d_model // 8,
        gate_rank=d_model // 16,
    )
    dm["da_rot"] = dm["da_sub"] // 4
    dm["mla_qk"] = dm["nope"] + dm["rope"]
    assert dm["da_head"] == LANES and dm["mla_v"] == LANES and dm["lru_bw"] == LANES
    return dm


def _h_layout(dm):
    bw = dm["bw"]
    off = dict(a_q=0, a_k=bw, a_v=2 * bw, b_gate=3 * bw, b_x=4 * bw)
    pos = 5 * bw
    off["c_kva"] = pos
    pos += dm["kv_lora"]
    off["g_lr"] = pos
    pos += dm["gate_rank"]
    off["c_kpe"] = pos
    pos += LANES
    pos = -(-pos // dm["q_lora"]) * dm["q_lora"]
    off["c_qa"] = pos
    pos += dm["q_lora"]
    for name, width in (("c_kva", dm["kv_lora"]), ("g_lr", dm["gate_rank"]), ("c_kpe", LANES), ("c_qa", dm["q_lora"])):
        assert off[name] % width == 0
    return off, pos


def _vmem(*nbytes):
    return min(int(sum(nbytes) * 1.25) + (2 << 20), VMEM_LIMIT_V7X)


def _nbytes(shape, dtype):
    return math.prod(shape) * jnp.dtype(dtype).itemsize


def _rms(x, g):
    return x * lax.rsqrt(jnp.mean(x * x, axis=-1, keepdims=True) + EPS) * g


def _norm_matmul_kernel(x_ref, g_ref, w_ref, o_ref, xn_ref):
    @pl.when(pl.program_id(1) == 0)
    def _():
        xn_ref[...] = _rms(x_ref[...], g_ref[...]).astype(BF16)

    o_ref[...] = jnp.dot(xn_ref[...], w_ref[...], preferred_element_type=F32).astype(o_ref.dtype)


def _norm_matmul(x, g, w, l, *, tm, tn):
    m, k = x.shape
    n = w.shape[-1]
    return pl.pallas_call(
        _norm_matmul_kernel,
        grid=(m // tm, n // tn),
        in_specs=[
            pl.BlockSpec((tm, k), lambda i, j: (i, 0)),
            pl.BlockSpec((None, 1, k), lambda i, j: (l, 0, 0)),
            pl.BlockSpec((None, k, tn), lambda i, j: (l, 0, j)),
        ],
        out_specs=pl.BlockSpec((tm, tn), lambda i, j: (i, j)),
        out_shape=jax.ShapeDtypeStruct((m, n), F32),
        scratch_shapes=[pltpu.VMEM((tm, k), BF16)],
        compiler_params=pltpu.CompilerParams(
            dimension_semantics=("parallel", "arbitrary"),
            vmem_limit_bytes=_vmem(2 * _nbytes((tm, k), F32), 2 * _nbytes((k, tn), BF16), 2 * _nbytes((tm, tn), F32),
                                   2 * _nbytes((tm, k), BF16)),
        ),
        name="in_proj",
    )(x, g, w)


def _ffn_up_kernel(x_ref, g_ref, wg_ref, wu_ref, o_ref, xn_ref):
    @pl.when(pl.program_id(1) == 0)
    def _():
        xn_ref[...] = _rms(x_ref[...], g_ref[...]).astype(BF16)

    xn = xn_ref[...]
    a = jnp.dot(xn, wg_ref[...], preferred_element_type=F32)
    b = jnp.dot(xn, wu_ref[...], preferred_element_type=F32)
    o_ref[...] = (a * jax.nn.sigmoid(a) * b).astype(o_ref.dtype)


def _ffn_up(x, g, wg, wu, l, *, tm, tn):
    m, k = x.shape
    n = wg.shape[-1]
    return pl.pallas_call(
        _ffn_up_kernel,
        grid=(m // tm, n // tn),
        in_specs=[
            pl.BlockSpec((tm, k), lambda i, j: (i, 0)),
            pl.BlockSpec((None, 1, k), lambda i, j: (l, 0, 0)),
            pl.BlockSpec((None, k, tn), lambda i, j: (l, 0, j)),
            pl.BlockSpec((None, k, tn), lambda i, j: (l, 0, j)),
        ],
        out_specs=pl.BlockSpec((tm, tn), lambda i, j: (i, j)),
        out_shape=jax.ShapeDtypeStruct((m, n), BF16),
        scratch_shapes=[pltpu.VMEM((tm, k), BF16)],
        compiler_params=pltpu.CompilerParams(
            dimension_semantics=("parallel", "arbitrary"),
            vmem_limit_bytes=_vmem(2 * _nbytes((tm, k), F32), 4 * _nbytes((k, tn), BF16), 2 * _nbytes((tm, tn), BF16),
                                   2 * _nbytes((tm, k), BF16), 3 * _nbytes((tm, tn), F32)),
        ),
        name="ffn_up",
    )(x, g, wg, wu)


def _matmul_res_kernel(a_ref, w_ref, r_ref, o_ref):
    o_ref[...] = r_ref[...] + jnp.dot(a_ref[...], w_ref[...], preferred_element_type=F32)


def _matmul_res(a, w, l, res, *, tm, tn, name):
    m, k = a.shape
    n = w.shape[-1]
    return pl.pallas_call(
        _matmul_res_kernel,
        grid=(m // tm, n // tn),
        in_specs=[
            pl.BlockSpec((tm, k), lambda i, j: (i, 0)),
            pl.BlockSpec((None, k, tn), lambda i, j: (l, 0, j)),
            pl.BlockSpec((tm, tn), lambda i, j: (i, j)),
        ],
        out_specs=pl.BlockSpec((tm, tn), lambda i, j: (i, j)),
        out_shape=jax.ShapeDtypeStruct((m, n), F32),
        input_output_aliases={2: 0},
        compiler_params=pltpu.CompilerParams(
            dimension_semantics=("parallel", "parallel"),
            vmem_limit_bytes=_vmem(2 * _nbytes((tm, k), BF16), 2 * _nbytes((k, tn), BF16), 5 * _nbytes((tm, tn), F32)),
        ),
        name=name,
    )(a, w, res)


def _branch_kernel(oa_ref, ob_ref, oc_ref, glr_ref, wbr_ref, wg0_ref, wg1_ref, wg2_ref, bg_ref, z_ref):
    glr = glr_ref[...].astype(BF16)
    acc = None
    for n, (o_ref, wg_ref) in enumerate(((oa_ref, wg0_ref), (ob_ref, wg1_ref), (oc_ref, wg2_ref))):
        y = jnp.dot(o_ref[...], wbr_ref[n], preferred_element_type=F32)
        gate = jax.nn.sigmoid(jnp.dot(glr, wg_ref[...], preferred_element_type=F32) + bg_ref[n:n + 1, :])
        acc = gate * y if acc is None else acc + gate * y
    z_ref[...] = acc.astype(z_ref.dtype)


def _branch(o_a, o_b, o_c, h, glr_off, w_br, w_gate, b_gate, l, *, tm, tn):
    m, bw = o_a.shape
    d = w_br.shape[-1]
    rank = w_gate.shape[1]
    nj = d // tn
    o_spec = pl.BlockSpec((tm, bw), lambda i, j: (i, 0))
    wg_specs = [pl.BlockSpec((None, rank, tn), functools.partial(lambda i, j, n: (l, 0, n * nj + j), n=n))
                for n in range(N_BRANCH)]
    return pl.pallas_call(
        _branch_kernel,
        grid=(m // tm, nj),
        in_specs=[
            o_spec, o_spec, o_spec,
            pl.BlockSpec((tm, rank), lambda i, j: (i, glr_off // rank)),
            pl.BlockSpec((None, N_BRANCH, bw, tn), lambda i, j: (l, 0, 0, j)),
            *wg_specs,
            pl.BlockSpec((None, N_BRANCH, tn), lambda i, j: (l, 0, j)),
        ],
        out_specs=pl.BlockSpec((tm, tn), lambda i, j: (i, j)),
        out_shape=jax.ShapeDtypeStruct((m, d), BF16),
        compiler_params=pltpu.CompilerParams(
            dimension_semantics=("parallel", "parallel"),
            vmem_limit_bytes=_vmem(6 * _nbytes((tm, bw), BF16), 2 * _nbytes((tm, rank), F32),
                                   2 * _nbytes((N_BRANCH, bw + rank, tn), BF16), 2 * _nbytes((tm, tn), BF16),
                                   4 * _nbytes((tm, tn), F32)),
        ),
        name="branch",
    )(o_a, o_b, o_c, h, w_br, w_gate, w_gate, w_gate, b_gate)


def _rope(y, tab, shift):
    c, s1, s2 = tab[:, 0:LANES], tab[:, LANES:2 * LANES], tab[:, 2 * LANES:3 * LANES]
    return y * c + pltpu.roll(y, LANES - shift, 1) * s1 + pltpu.roll(y, shift, 1) * s2


def _da_prep_kernel(q_ref, k_ref, v_ref, gq_ref, gk_ref, tab_ref, qo_ref, ko_ref, vo_ref, *, sub, rot, scale):
    tab = tab_ref[...]
    lo = lax.broadcasted_iota(jnp.int32, (q_ref.shape[0], LANES), 1) < sub

    def prep(x_ref, g_ref, o_ref, out_scale):
        for h in range(DA_HEADS):
            cols = slice(h * LANES, (h + 1) * LANES)
            x = x_ref[:, cols]
            sq = x * x
            ss0 = jnp.sum(jnp.where(lo, sq, 0.0), axis=-1, keepdims=True)
            ss1 = jnp.sum(jnp.where(lo, 0.0, sq), axis=-1, keepdims=True)
            y = x * lax.rsqrt(jnp.where(lo, ss0, ss1) * (1.0 / sub) + EPS) * g_ref[...]
            o_ref[:, cols] = (_rope(y, tab, rot // 2) * out_scale).astype(o_ref.dtype)

    prep(q_ref, gq_ref, qo_ref, scale)
    prep(k_ref, gk_ref, ko_ref, 1.0)
    vo_ref[...] = v_ref[...].astype(vo_ref.dtype)


def _da_prep(h, off, gq, gk, tab, l, dm, *, tm):
    m = h.shape[0]
    bw = dm["bw"]
    col = lambda name: pl.BlockSpec((tm, bw), functools.partial(lambda i, c: (i, c), c=off[name] // bw))
    g_spec = pl.BlockSpec((None, 1, LANES), lambda i: (l, 0, 0))
    out = jax.ShapeDtypeStruct((m, bw), BF16)
    o_spec = pl.BlockSpec((tm, bw), lambda i: (i, 0))
    return pl.pallas_call(
        functools.partial(_da_prep_kernel, sub=dm["da_sub"], rot=dm["da_rot"], scale=dm["da_sub"] ** -0.5),
        grid=(m // tm,),
        in_specs=[col("a_q"), col("a_k"), col("a_v"), g_spec, g_spec,
                  pl.BlockSpec((tm, 3 * LANES), lambda i: (i, 0))],
        out_specs=[o_spec, o_spec, o_spec],
        out_shape=[out, out, out],
        compiler_params=pltpu.CompilerParams(
            dimension_semantics=("parallel",),
            vmem_limit_bytes=_vmem(6 * _nbytes((tm, bw), F32), 6 * _nbytes((tm, bw), BF16), 2 * _nbytes((tm, 3 * LANES), F32)),
        ),
        name="da_prep",
    )(h, h, h, gq, gk, tab)


def _flash(q, k_ref, v_ref, tk):
    tq = q.shape[0]
    dv = v_ref.shape[-1]

    def body(c, carry):
        m, l, acc = carry
        start = pl.multiple_of(c * tk, tk)
        kc = k_ref[pl.ds(start, tk), :]
        vc = v_ref[pl.ds(start, tk), :]
        s = lax.dot_general(q, kc, (((1,), (1,)), ((), ())), preferred_element_type=F32)
        m_new = jnp.maximum(m, jnp.max(s, axis=-1, keepdims=True))
        alpha = jnp.exp(m - m_new)
        p = jnp.exp(s - m_new)
        l = alpha * l + jnp.sum(p, axis=-1, keepdims=True)
        acc = alpha * acc + jnp.dot(p.astype(BF16), vc, preferred_element_type=F32)
        return m_new, l, acc

    init = (jnp.full((tq, 1), -jnp.inf, F32), jnp.zeros((tq, 1), F32), jnp.zeros((tq, dv), F32))
    _, l, acc = lax.fori_loop(0, k_ref.shape[0] // tk, body, init)
    return acc / l


def _da_attn_kernel(lam_ref, q_ref, k_ref, v_ref, g_ref, o_ref, *, tk, sub, lam_init):
    q = q_ref[...]
    lo = lax.broadcasted_iota(jnp.int32, q.shape, 1) < sub
    zero = jnp.zeros_like(q)
    o0 = _flash(jnp.where(lo, q, zero), k_ref, v_ref, tk)
    o1 = _flash(jnp.where(lo, zero, q), k_ref, v_ref, tk)
    lp = lam_ref[...]
    lam = (jnp.exp(jnp.sum(lp[0:1] * lp[1:2], axis=-1, keepdims=True))
           - jnp.exp(jnp.sum(lp[2:3] * lp[3:4], axis=-1, keepdims=True)) + lam_init)
    o = o0 - lam * o1
    o_ref[...] = (_rms(o, g_ref[...]) * (1.0 - lam_init)).astype(o_ref.dtype)


def _da_attn(q, k, v, lam_p, g, l, batch, seq, dm, lam_init, *, tq, tk):
    m, bw = q.shape
    nq = seq // tq
    return pl.pallas_call(
        functools.partial(_da_attn_kernel, tk=tk, sub=dm["da_sub"], lam_init=lam_init),
        grid=(batch, DA_HEADS, nq),
        in_specs=[
            pl.BlockSpec((None, 4, dm["da_sub"]), lambda b, h, i: (l, 0, 0)),
            pl.BlockSpec((tq, LANES), lambda b, h, i: (b * nq + i, h)),
            pl.BlockSpec((seq, LANES), lambda b, h, i: (b, h)),
            pl.BlockSpec((seq, LANES), lambda b, h, i: (b, h)),
            pl.BlockSpec((None, 1, LANES), lambda b, h, i: (l, 0, 0)),
        ],
        out_specs=pl.BlockSpec((tq, LANES), lambda b, h, i: (b * nq + i, h)),
        out_shape=jax.ShapeDtypeStruct((m, bw), BF16),
        compiler_params=pltpu.CompilerParams(
            dimension_semantics=("parallel", "parallel", "arbitrary"),
            vmem_limit_bytes=_vmem(4 * _nbytes((seq, LANES), BF16), 4 * _nbytes((tq, LANES), BF16),
                                   6 * _nbytes((tq, tk), F32)),
        ),
        name="da_attn",
    )(lam_p, q, k, v, g)


def _mla_attn_kernel(q_ref, k_ref, v_ref, o_ref, *, tk):
    o_ref[...] = _flash(q_ref[...], k_ref, v_ref, tk).astype(o_ref.dtype)


def _mla_attn(q, k, v, batch, seq, *, tq, tk):
    m = q.shape[0]
    dk = q.shape[1] // MLA_HEADS
    nq = seq // tq
    return pl.pallas_call(
        functools.partial(_mla_attn_kernel, tk=tk),
        grid=(batch, MLA_HEADS, nq),
        in_specs=[
            pl.BlockSpec((tq, dk), lambda b, h, i: (b * nq + i, h)),
            pl.BlockSpec((seq, dk), lambda b, h, i: (b, h)),
            pl.BlockSpec((seq, LANES), lambda b, h, i: (b, h)),
        ],
        out_specs=pl.BlockSpec((tq, LANES), lambda b, h, i: (b * nq + i, h)),
        out_shape=jax.ShapeDtypeStruct((m, MLA_HEADS * LANES), BF16),
        compiler_params=pltpu.CompilerParams(
            dimension_semantics=("parallel", "parallel", "arbitrary"),
            vmem_limit_bytes=_vmem(2 * _nbytes((seq, dk + LANES), BF16), 4 * _nbytes((tq, dk), BF16),
                                   6 * _nbytes((tq, tk), F32)),
        ),
        name="mla_attn",
    )(q, k, v)


def _mla_q_kernel(cq_ref, ga_ref, w_ref, gq_ref, tab_ref, o_ref, *, qk, rope, scale):
    xn = _rms(cq_ref[...], ga_ref[...]).astype(BF16)
    q = jnp.dot(xn, w_ref[...], preferred_element_type=F32)
    tab = tab_ref[...]
    hw = 2 * LANES
    for h in range(MLA_HEADS):
        blk = q[:, h * hw:(h + 1) * hw]
        sc = lax.rsqrt(jnp.sum(blk * blk, axis=-1, keepdims=True) * (1.0 / qk) + EPS)
        y = blk * sc * gq_ref[:, h * hw:(h + 1) * hw]
        o_ref[:, h * hw:h * hw + LANES] = (_rope(y[:, :LANES], tab, rope // 2) * scale).astype(o_ref.dtype)
        o_ref[:, h * hw + LANES:(h + 1) * hw] = (y[:, LANES:] * scale).astype(o_ref.dtype)


def _mla_q(h, off, ga, w, gq, tab, l, dm, *, tm):
    m = h.shape[0]
    r = dm["q_lora"]
    n = w.shape[-1]
    return pl.pallas_call(
        functools.partial(_mla_q_kernel, qk=dm["mla_qk"], rope=dm["rope"], scale=dm["mla_qk"] ** -0.5),
        grid=(m // tm,),
        in_specs=[
            pl.BlockSpec((tm, r), lambda i: (i, off["c_qa"] // r)),
            pl.BlockSpec((None, 1, r), lambda i: (l, 0, 0)),
            pl.BlockSpec((None, r, n), lambda i: (l, 0, 0)),
            pl.BlockSpec((None, 1, n), lambda i: (l, 0, 0)),
            pl.BlockSpec((tm, 3 * LANES), lambda i: (i, 0)),
        ],
        out_specs=pl.BlockSpec((tm, n), lambda i: (i, 0)),
        out_shape=jax.ShapeDtypeStruct((m, n), BF16),
        compiler_params=pltpu.CompilerParams(
            dimension_semantics=("parallel",),
            vmem_limit_bytes=_vmem(2 * _nbytes((tm, r), F32), 2 * _nbytes((r, n), BF16), 2 * _nbytes((tm, n), BF16),
                                   3 * _nbytes((tm, n), F32)),
        ),
        name="mla_q",
    )(h, ga, w, gq, tab)


def _mla_kv_kernel(ckv_ref, kpe_ref, ga_ref, w_ref, gpe_ref, gn_ref, tab_ref, k_ref, v_ref, *, qk, rope):
    xn = _rms(ckv_ref[...], ga_ref[...]).astype(BF16)
    kv = jnp.dot(xn, w_ref[...], preferred_element_type=F32)
    nv = MLA_HEADS * LANES
    v_ref[...] = kv[:, nv:].astype(v_ref.dtype)
    kpe = kpe_ref[...]
    ss_pe = jnp.sum(kpe * kpe, axis=-1, keepdims=True)
    pe = _rope(kpe * gpe_ref[...], tab_ref[...], rope // 2)
    hw = 2 * LANES
    for h in range(MLA_HEADS):
        kn = kv[:, h * LANES:(h + 1) * LANES]
        sc = lax.rsqrt((ss_pe + jnp.sum(kn * kn, axis=-1, keepdims=True)) * (1.0 / qk) + EPS)
        k_ref[:, h * hw:h * hw + LANES] = (pe * sc).astype(k_ref.dtype)
        k_ref[:, h * hw + LANES:(h + 1) * hw] = (kn * gn_ref[:, h * LANES:(h + 1) * LANES] * sc).astype(k_ref.dtype)


def _mla_kv(h, off, ga, w, gpe, gn, tab, l, dm, *, tm):
    m = h.shape[0]
    r = dm["kv_lora"]
    n = w.shape[-1]
    nv = MLA_HEADS * LANES
    return pl.pallas_call(
        functools.partial(_mla_kv_kernel, qk=dm["mla_qk"], rope=dm["rope"]),
        grid=(m // tm,),
        in_specs=[
            pl.BlockSpec((tm, r), lambda i: (i, off["c_kva"] // r)),
            pl.BlockSpec((tm, LANES), lambda i: (i, off["c_kpe"] // LANES)),
            pl.BlockSpec((None, 1, r), lambda i: (l, 0, 0)),
            pl.BlockSpec((None, r, n), lambda i: (l, 0, 0)),
            pl.BlockSpec((None, 1, LANES), lambda i: (l, 0, 0)),
            pl.BlockSpec((None, 1, nv), lambda i: (l, 0, 0)),
            pl.BlockSpec((tm, 3 * LANES), lambda i: (i, 0)),
        ],
        out_specs=[pl.BlockSpec((tm, 2 * nv), lambda i: (i, 0)), pl.BlockSpec((tm, nv), lambda i: (i, 0))],
        out_shape=[jax.ShapeDtypeStruct((m, 2 * nv), BF16), jax.ShapeDtypeStruct((m, nv), BF16)],
        compiler_params=pltpu.CompilerParams(
            dimension_semantics=("parallel",),
            vmem_limit_bytes=_vmem(2 * _nbytes((tm, r), F32), 2 * _nbytes((r, n), BF16), 2 * _nbytes((tm, 3 * nv), BF16),
                                   3 * _nbytes((tm, n), F32)),
        ),
        name="mla_kv",
    )(h, h, ga, w, gpe, gn, tab)


def _lru_kernel(u_ref, up_ref, un_ref, gt_ref, cw_ref, cb_ref, gw_ref, gb_ref, lam_ref, o_ref,
                ue_ref, a_ref, b_ref, h_ref, hb_ref, *, nt, ts):
    t = pl.program_id(1)
    backward = t < nt
    it = jnp.where(backward, nt - 1 - t, t - nt)
    width = u_ref.shape[1]
    halo = SUBLANES

    @pl.when((t == 0) | (t == nt))
    def _():
        h_ref[...] = jnp.zeros_like(h_ref)

    ue_ref[0:halo, :] = jnp.where(it > 0, up_ref[...], 0.0)
    ue_ref[halo:halo + ts, :] = u_ref[...]
    ue_ref[halo + ts:2 * halo + ts, :] = jnp.where(it < nt - 1, un_ref[...], 0.0)
    cw = cw_ref[...]
    pad_l = CONV_W // 2
    uc = cb_ref[...] + sum(cw[j:j + 1] * ue_ref[halo - pad_l + j:halo - pad_l + j + ts, :] for j in range(CONV_W))

    ucb = uc.astype(BF16)
    pre_r, pre_i = [], []
    for n in range(LRU_BLOCKS):
        g2 = jnp.dot(ucb[:, n * LANES:(n + 1) * LANES], gw_ref[n], preferred_element_type=F32)
        pre_r.append(g2[:, :LANES])
        pre_i.append(g2[:, LANES:])
    r = jax.nn.sigmoid(jnp.concatenate(pre_r, axis=1) + gb_ref[0:1, :])
    i = jax.nn.sigmoid(jnp.concatenate(pre_i, axis=1) + gb_ref[1:2, :])
    log_a = (-LRU_C * r) * jax.nn.softplus(-lam_ref[...])
    a_ref[...] = jnp.exp(log_a)
    th = jnp.tanh(log_a)
    b_ref[...] = jnp.sqrt(-2.0 * th / (1.0 - th)) * i * uc

    row = lax.broadcasted_iota(jnp.int32, (SUBLANES, width), 0)
    ng = ts // SUBLANES

    def scan_group(g, h, reverse):
        rows = pl.ds(pl.multiple_of(g * SUBLANES, SUBLANES), SUBLANES)
        a, b = a_ref[rows, :], b_ref[rows, :]
        for d in (1, 2, 4):
            shift = SUBLANES - d if reverse else d
            keep = (row < SUBLANES - d) if reverse else (row >= d)
            b = jnp.where(keep, a * pltpu.roll(b, shift, 0) + b, b)
            a = jnp.where(keep, a * pltpu.roll(a, shift, 0), a)
        return rows, a * h + b

    @pl.when(backward)
    def _():
        base = pl.multiple_of(it * ts, ts)

        def body(k, h):
            rows, hs = scan_group(ng - 1 - k, h, True)
            hb_ref[pl.ds(base + rows.start, SUBLANES), :] = hs
            return hs[0:1, :]

        h_ref[...] = lax.fori_loop(0, ng, body, h_ref[...])

    @pl.when(jnp.logical_not(backward))
    def _():
        def body(k, h):
            rows, hs = scan_group(k, h, False)
            b_ref[rows, :] = hs
            return hs[SUBLANES - 1:SUBLANES, :]

        h_ref[...] = lax.fori_loop(0, ng, body, h_ref[...])
        hsum = b_ref[...] + hb_ref[pl.ds(pl.multiple_of(it * ts, ts), ts), :]
        o_ref[...] = (hsum * jax.nn.gelu(gt_ref[...])).astype(o_ref.dtype)


def _lru(h, off, conv_w, conv_b, gate_w, gate_b, lam, l, batch, seq, dm, *, ts):
    m = h.shape[0]
    bw = dm["bw"]
    nt = seq // ts
    hb = ts // SUBLANES
    n_hblk = m // SUBLANES
    ucol, gcol = off["b_x"] // bw, off["b_gate"] // bw

    def tile(t):
        return jnp.where(t < nt, nt - 1 - t, t - nt)

    def direction(t):
        return jnp.where(t < nt, 1, 0)

    return pl.pallas_call(
        functools.partial(_lru_kernel, nt=nt, ts=ts),
        grid=(batch, 2 * nt),
        in_specs=[
            pl.BlockSpec((ts, bw), lambda b, t: (b * nt + tile(t), ucol)),
            pl.BlockSpec((SUBLANES, bw), lambda b, t: (jnp.maximum((b * nt + tile(t)) * hb - 1, 0), ucol)),
            pl.BlockSpec((SUBLANES, bw), lambda b, t: (jnp.minimum((b * nt + tile(t) + 1) * hb, n_hblk - 1), ucol)),
            pl.BlockSpec((ts, bw), lambda b, t: (b * nt + tile(t), gcol)),
            pl.BlockSpec((None, CONV_W, bw), lambda b, t: (l, 0, 0)),
            pl.BlockSpec((None, 1, bw), lambda b, t: (l, 0, 0)),
            pl.BlockSpec((None, None, LRU_BLOCKS, LANES, 2 * LANES), lambda b, t: (l, direction(t), 0, 0, 0)),
            pl.BlockSpec((None, None, 2, bw), lambda b, t: (l, direction(t), 0, 0)),
            pl.BlockSpec((None, None, 1, bw), lambda b, t: (l, direction(t), 0, 0)),
        ],
        out_specs=pl.BlockSpec((ts, bw), lambda b, t: (b * nt + jnp.maximum(t - nt, 0), 0)),
        out_shape=jax.ShapeDtypeStruct((m, bw), BF16),
        scratch_shapes=[
            pltpu.VMEM((ts + 2 * SUBLANES, bw), F32),
            pltpu.VMEM((ts, bw), F32),
            pltpu.VMEM((ts, bw), F32),
            pltpu.VMEM((1, bw), F32),
            pltpu.VMEM((seq, bw), F32),
        ],
        compiler_params=pltpu.CompilerParams(
            dimension_semantics=("parallel", "arbitrary"),
            vmem_limit_bytes=_vmem(_nbytes((seq, bw), F32), 16 * _nbytes((ts, bw), F32)),
        ),
        name="lru",
    )(h, h, h, h, conv_w, conv_b, gate_w, gate_b, lam)


def _rope_table(positions, rot_dim, period):
    half = rot_dim // 2
    inv = ROPE_THETA ** (-jnp.arange(0, rot_dim, 2, dtype=F32) / rot_dim)
    ang = positions.astype(F32)[..., None] * inv
    c, s = jnp.cos(ang), jnp.sin(ang)
    one = jnp.ones(ang.shape[:-1] + (period - rot_dim,), F32)
    zero = jnp.zeros(ang.shape[:-1] + (period - half,), F32)
    parts = [jnp.concatenate([c, c, one], -1),
             jnp.concatenate([-s, zero], -1),
             jnp.concatenate([zero[..., :half], s, zero[..., :period - rot_dim]], -1)]
    tab = jnp.concatenate([jnp.tile(p, LANES // period) for p in parts], -1)
    return tab.reshape(-1, 3 * LANES)


def kernel(x, positions, mixer_norm_g, w_in, da_q_norm_g, da_k_norm_g, da_lambda, da_subln_g, lru_conv_w, lru_conv_b, lru_gate_w, lru_gate_b, lru_L, mla_q_a_norm_g, mla_w_uq, mla_kv_a_norm_g, mla_w_ukv, mla_q_norm_g, mla_k_norm_g, w_gate_up, b_gate, w_branch_out, w_out, ffn_norm_g, w_ffn_gate, w_ffn_up, w_ffn_down):
    batch, seq, d = x.shape
    depth = w_in.shape[0]
    dm = _dims(d)
    off, h_cols = _h_layout(dm)
    bw, rope, nope = dm["bw"], dm["rope"], dm["nope"]
    m = batch * seq

    seg = {}
    pos = 0
    for name, width in (("a_q", bw), ("a_k", bw), ("a_v", bw), ("b_gate", bw), ("b_x", bw), ("c_qa", dm["q_lora"]),
                        ("c_kva", dm["kv_lora"]), ("c_kpe", rope), ("g_lr", dm["gate_rank"])):
        seg[name] = w_in[..., pos:pos + width]
        pos += width
    order = sorted(off, key=off.get)
    pieces, pos = [], 0
    for name in order:
        if off[name] > pos:
            pieces.append(jnp.zeros(w_in.shape[:2] + (off[name] - pos,), w_in.dtype))
        pieces.append(seg[name])
        pos = off[name] + seg[name].shape[-1]
    if h_cols > pos:
        pieces.append(jnp.zeros(w_in.shape[:2] + (h_cols - pos,), w_in.dtype))
    w_in_p = jnp.concatenate(pieces, axis=-1).astype(BF16)

    uq = mla_w_uq.reshape(depth, dm["q_lora"], MLA_HEADS, dm["mla_qk"])
    w_uq_p = jnp.concatenate([uq[..., :rope], jnp.zeros(uq.shape[:3] + (LANES - rope,), uq.dtype), uq[..., rope:]], -1)
    w_uq_p = w_uq_p.reshape(depth, dm["q_lora"], MLA_HEADS * 2 * LANES).astype(BF16)
    ukv = mla_w_ukv.reshape(depth, dm["kv_lora"], MLA_HEADS, 2, nope)
    w_ukv_p = ukv.transpose(0, 1, 3, 2, 4).reshape(depth, dm["kv_lora"], 2 * MLA_HEADS * nope).astype(BF16)
    pad = jnp.zeros((depth, LANES - rope), F32)
    gq_mla = jnp.tile(jnp.concatenate([mla_q_norm_g[:, :rope], pad, mla_q_norm_g[:, rope:]], -1), (1, MLA_HEADS))[:, None]
    gpe_mla = jnp.concatenate([mla_k_norm_g[:, :rope], pad], -1)[:, None]
    gn_mla = jnp.tile(mla_k_norm_g[:, rope:], (1, MLA_HEADS))[:, None]
    gq_da = jnp.tile(da_q_norm_g, (1, 2))[:, None]
    gk_da = jnp.tile(da_k_norm_g, (1, 2))[:, None]
    gate_w_p = lru_gate_w.transpose(0, 1, 3, 4, 2, 5).reshape(depth, 2, LRU_BLOCKS, LANES, 2 * LANES).astype(BF16)
    w_gate_p = w_gate_up.astype(BF16)
    w_br_p = w_branch_out.astype(BF16)
    w_out_p = w_out.astype(BF16)
    w_fg_p, w_fu_p, w_fd_p = w_ffn_gate.astype(BF16), w_ffn_up.astype(BF16), w_ffn_down.astype(BF16)

    tab_da = _rope_table(positions, dm["da_rot"], dm["da_sub"])
    tab_mla = _rope_table(positions, rope, LANES)

    xf = x.reshape(m, d)
    for l in range(depth):
        lam_init = 0.8 - 0.6 * math.exp(-0.3 * l)
        h = _norm_matmul(xf, mixer_norm_g[:, None], w_in_p, l, tm=512, tn=h_cols // 9)
        q_a, k_a, v_a = _da_prep(h, off, gq_da, gk_da, tab_da, l, dm, tm=512)
        o_a = _da_attn(q_a, k_a, v_a, da_lambda, da_subln_g[:, None], l, batch, seq, dm, lam_init, tq=256, tk=512)
        o_b = _lru(h, off, lru_conv_w, lru_conv_b[:, None], gate_w_p, lru_gate_b, lru_L[:, :, None], l, batch, seq, dm,
                   ts=256)
        q_c = _mla_q(h, off, mla_q_a_norm_g[:, None], w_uq_p, gq_mla, tab_mla, l, dm, tm=512)
        k_c, v_c = _mla_kv(h, off, mla_kv_a_norm_g[:, None], w_ukv_p, gpe_mla, gn_mla, tab_mla, l, dm, tm=512)
        o_c = _mla_attn(q_c, k_c, v_c, batch, seq, tq=256, tk=512)
        z = _branch(o_a, o_b, o_c, h, off["g_lr"], w_br_p, w_gate_p, b_gate, l, tm=512, tn=512)
        xf = _matmul_res(z, w_out_p, l, xf, tm=1024, tn=512, name="out_proj")
        a = _ffn_up(xf, ffn_norm_g[:, None], w_fg_p, w_fu_p, l, tm=512, tn=256)
        xf = _matmul_res(a, w_fd_p, l, xf, tm=512, tn=256, name="ffn_down")
    return xf.reshape(batch, seq, d)
```

```python
import functools
import math

import jax
import jax.numpy as jnp
from jax import lax
from jax.experimental import pallas as pl
from jax.experimental.pallas import tpu as pltpu

F32 = jnp.float32
BF16 = jnp.bfloat16

LANES = 128
SUBLANES = 8
VMEM_LIMIT_V7X = 60000 * 1024

EPS = 1e-6
LOG2_E = math.log2(math.e)
ROPE_THETA = 500000.0
LRU_C = 8.0
CONV_W = 4
N_BRANCH = 3
DA_HEADS = 8
MLA_HEADS = 8
LRU_BLOCKS = 8


def _dims(d_model):
    bw = d_model // 4
    dm = dict(
        bw=bw,
        da_sub=bw // (2 * DA_HEADS),
        da_head=bw // DA_HEADS,
        lru_bw=bw // LRU_BLOCKS,
        nope=128,
        rope=64,
        mla_v=bw // MLA_HEADS,
        q_lora=3 * d_model // 16,
        kv_lora=d_model // 8,
        gate_rank=d_model // 16,
    )
    dm["da_rot"] = dm["da_sub"] // 4
    dm["mla_qk"] = dm["nope"] + dm["rope"]
    assert dm["da_head"] == LANES and dm["mla_v"] == LANES and dm["lru_bw"] == LANES
    return dm


def _h_layout(dm):
    bw = dm["bw"]
    off = dict(a_q=0, a_k=bw, a_v=2 * bw, b_gate=3 * bw, b_x=4 * bw)
    pos = 5 * bw
    off["c_kva"] = pos
    pos += dm["kv_lora"]
    off["g_lr"] = pos
    pos += dm["gate_rank"]
    off["c_kpe"] = pos
    pos += LANES
    pos = -(-pos // dm["q_lora"]) * dm["q_lora"]
    off["c_qa"] = pos
    pos += dm["q_lora"]
    for name, width in (("c_kva", dm["kv_lora"]), ("g_lr", dm["gate_rank"]), ("c_kpe", LANES), ("c_qa", dm["q_lora"])):
        assert off[name] % width == 0
    return off, pos


def _vmem(*nbytes):
    return min(int(sum(nbytes) * 1.25) + (2 << 20), VMEM_LIMIT_V7X)


def _nbytes(shape, dtype):
    return math.prod(shape) * jnp.dtype(dtype).itemsize


def _rms(x, g):
    return x * lax.rsqrt(jnp.mean(x * x, axis=-1, keepdims=True) + EPS) * g


NORM_ROWS = 256


def _norm_rows(x_ref, g_ref, xn_ref):
    @pl.when(pl.program_id(1) == 0)
    def _():
        g = g_ref[...]

        def body(r, carry):
            rows = pl.ds(pl.multiple_of(r * NORM_ROWS, NORM_ROWS), NORM_ROWS)
            xn_ref[rows, :] = _rms(x_ref[rows, :], g).astype(BF16)
            return carry

        lax.fori_loop(0, x_ref.shape[0] // NORM_ROWS, body, 0)


def _norm_matmul_kernel(x_ref, g_ref, w_ref, o_ref, xn_ref):
    _norm_rows(x_ref, g_ref, xn_ref)
    o_ref[...] = jnp.dot(xn_ref[...], w_ref[...], preferred_element_type=F32).astype(o_ref.dtype)


def _norm_matmul(x, g, w, l, *, tm, tn):
    m, k = x.shape
    n = w.shape[-1]
    return pl.pallas_call(
        _norm_matmul_kernel,
        grid=(m // tm, n // tn),
        in_specs=[
            pl.BlockSpec((tm, k), lambda i, j: (i, 0), pipeline_mode=pl.Buffered(1)),
            pl.BlockSpec((None, 1, k), lambda i, j: (l, 0, 0)),
            pl.BlockSpec((None, k, tn), lambda i, j: (l, 0, j)),
        ],
        out_specs=pl.BlockSpec((tm, tn), lambda i, j: (i, j)),
        out_shape=jax.ShapeDtypeStruct((m, n), F32),
        scratch_shapes=[pltpu.VMEM((tm, k), BF16)],
        compiler_params=pltpu.CompilerParams(
            dimension_semantics=("parallel", "arbitrary"),
            vmem_limit_bytes=_vmem(_nbytes((tm, k), F32), 2 * _nbytes((k, tn), BF16), 2 * _nbytes((tm, tn), F32),
                                   _nbytes((tm, k), BF16), 4 * _nbytes((NORM_ROWS, k), F32)),
        ),
        name="in_proj",
    )(x, g, w)


def _ffn_up_kernel(x_ref, g_ref, wg_ref, wu_ref, o_ref, xn_ref):
    _norm_rows(x_ref, g_ref, xn_ref)
    xn = xn_ref[...]
    a = jnp.dot(xn, wg_ref[...], preferred_element_type=F32)
    b = jnp.dot(xn, wu_ref[...], preferred_element_type=F32)
    o_ref[...] = (a * jax.nn.sigmoid(a) * b).astype(o_ref.dtype)


def _ffn_up(x, g, wg, wu, l, *, tm, tn):
    m, k = x.shape
    n = wg.shape[-1]
    return pl.pallas_call(
        _ffn_up_kernel,
        grid=(m // tm, n // tn),
        in_specs=[
            pl.BlockSpec((tm, k), lambda i, j: (i, 0), pipeline_mode=pl.Buffered(1)),
            pl.BlockSpec((None, 1, k), lambda i, j: (l, 0, 0)),
            pl.BlockSpec((None, k, tn), lambda i, j: (l, 0, j)),
            pl.BlockSpec((None, k, tn), lambda i, j: (l, 0, j)),
        ],
        out_specs=pl.BlockSpec((tm, tn), lambda i, j: (i, j)),
        out_shape=jax.ShapeDtypeStruct((m, n), BF16),
        scratch_shapes=[pltpu.VMEM((tm, k), BF16)],
        compiler_params=pltpu.CompilerParams(
            dimension_semantics=("parallel", "arbitrary"),
            vmem_limit_bytes=_vmem(_nbytes((tm, k), F32), 4 * _nbytes((k, tn), BF16), 2 * _nbytes((tm, tn), BF16),
                                   _nbytes((tm, k), BF16), 4 * _nbytes((NORM_ROWS, k), F32), 4 * _nbytes((tm, tn), F32)),
        ),
        name="ffn_up",
    )(x, g, wg, wu)


def _matmul_res_kernel(a_ref, w_ref, r_ref, o_ref):
    o_ref[...] = r_ref[...] + jnp.dot(a_ref[...], w_ref[...], preferred_element_type=F32)


def _matmul_res(a, w, l, res, *, tm, tn, name):
    m, k = a.shape
    n = w.shape[-1]
    return pl.pallas_call(
        _matmul_res_kernel,
        grid=(m // tm, n // tn),
        in_specs=[
            pl.BlockSpec((tm, k), lambda i, j: (i, 0)),
            pl.BlockSpec((None, k, tn), lambda i, j: (l, 0, j)),
            pl.BlockSpec((tm, tn), lambda i, j: (i, j)),
        ],
        out_specs=pl.BlockSpec((tm, tn), lambda i, j: (i, j)),
        out_shape=jax.ShapeDtypeStruct((m, n), F32),
        input_output_aliases={2: 0},
        compiler_params=pltpu.CompilerParams(
            dimension_semantics=("parallel", "parallel"),
            vmem_limit_bytes=_vmem(2 * _nbytes((tm, k), BF16), 2 * _nbytes((k, tn), BF16), 5 * _nbytes((tm, tn), F32)),
        ),
        name=name,
    )(a, w, res)


def _branch_kernel(oa_ref, ob_ref, oc_ref, glr_ref, wbr_ref, wg0_ref, wg1_ref, wg2_ref, bg_ref, z_ref):
    glr = glr_ref[...].astype(BF16)
    acc = None
    for n, (o_ref, wg_ref) in enumerate(((oa_ref, wg0_ref), (ob_ref, wg1_ref), (oc_ref, wg2_ref))):
        y = jnp.dot(o_ref[...], wbr_ref[n], preferred_element_type=F32)
        gate = jax.nn.sigmoid(jnp.dot(glr, wg_ref[...], preferred_element_type=F32) + bg_ref[n:n + 1, :])
        acc = gate * y if acc is None else acc + gate * y
    z_ref[...] = acc.astype(z_ref.dtype)


def _branch(o_a, o_b, o_c, h, glr_off, w_br, w_gate, b_gate, l, *, tm, tn):
    m, bw = o_a.shape
    d = w_br.shape[-1]
    rank = w_gate.shape[1]
    nj = d // tn
    o_spec = pl.BlockSpec((tm, bw), lambda i, j: (i, 0))
    wg_specs = [pl.BlockSpec((None, rank, tn), functools.partial(lambda i, j, n: (l, 0, n * nj + j), n=n))
                for n in range(N_BRANCH)]
    return pl.pallas_call(
        _branch_kernel,
        grid=(m // tm, nj),
        in_specs=[
            o_spec, o_spec, o_spec,
            pl.BlockSpec((tm, rank), lambda i, j: (i, glr_off // rank)),
            pl.BlockSpec((None, N_BRANCH, bw, tn), lambda i, j: (l, 0, 0, j)),
            *wg_specs,
            pl.BlockSpec((None, N_BRANCH, tn), lambda i, j: (l, 0, j)),
        ],
        out_specs=pl.BlockSpec((tm, tn), lambda i, j: (i, j)),
        out_shape=jax.ShapeDtypeStruct((m, d), BF16),
        compiler_params=pltpu.CompilerParams(
            dimension_semantics=("parallel", "parallel"),
            vmem_limit_bytes=_vmem(6 * _nbytes((tm, bw), BF16), 2 * _nbytes((tm, rank), F32),
                                   2 * _nbytes((N_BRANCH, bw + rank, tn), BF16), 2 * _nbytes((tm, tn), BF16),
                                   4 * _nbytes((tm, tn), F32)),
        ),
        name="branch",
    )(o_a, o_b, o_c, h, w_br, w_gate, w_gate, w_gate, b_gate)


def _rope(y, tab, shift):
    c, s1, s2 = tab[:, 0:LANES], tab[:, LANES:2 * LANES], tab[:, 2 * LANES:3 * LANES]
    return y * c + pltpu.roll(y, LANES - shift, 1) * s1 + pltpu.roll(y, shift, 1) * s2


def _da_prep_kernel(q_ref, k_ref, v_ref, gq_ref, gk_ref, tab_ref, qo_ref, ko_ref, vo_ref, *, sub, rot, scale):
    tab = tab_ref[...]
    lo = lax.broadcasted_iota(jnp.int32, (q_ref.shape[0], LANES), 1) < sub

    def prep(x_ref, g_ref, o_ref, out_scale):
        for h in range(DA_HEADS):
            cols = slice(h * LANES, (h + 1) * LANES)
            x = x_ref[:, cols]
            sq = x * x
            ss0 = jnp.sum(jnp.where(lo, sq, 0.0), axis=-1, keepdims=True)
            ss1 = jnp.sum(jnp.where(lo, 0.0, sq), axis=-1, keepdims=True)
            y = x * lax.rsqrt(jnp.where(lo, ss0, ss1) * (1.0 / sub) + EPS) * g_ref[...]
            o_ref[:, cols] = (_rope(y, tab, rot // 2) * out_scale).astype(o_ref.dtype)

    prep(q_ref, gq_ref, qo_ref, scale)
    prep(k_ref, gk_ref, ko_ref, 1.0)
    vo_ref[...] = v_ref[...].astype(vo_ref.dtype)


def _da_prep(h, off, gq, gk, tab, l, dm, *, tm):
    m = h.shape[0]
    bw = dm["bw"]
    col = lambda name: pl.BlockSpec((tm, bw), functools.partial(lambda i, c: (i, c), c=off[name] // bw))
    g_spec = pl.BlockSpec((None, 1, LANES), lambda i: (l, 0, 0))
    out = jax.ShapeDtypeStruct((m, bw), BF16)
    o_spec = pl.BlockSpec((tm, bw), lambda i: (i, 0))
    return pl.pallas_call(
        functools.partial(_da_prep_kernel, sub=dm["da_sub"], rot=dm["da_rot"], scale=dm["da_sub"] ** -0.5 * LOG2_E),
        grid=(m // tm,),
        in_specs=[col("a_q"), col("a_k"), col("a_v"), g_spec, g_spec,
                  pl.BlockSpec((tm, 3 * LANES), lambda i: (i, 0))],
        out_specs=[o_spec, o_spec, o_spec],
        out_shape=[out, out, out],
        compiler_params=pltpu.CompilerParams(
            dimension_semantics=("parallel",),
            vmem_limit_bytes=_vmem(6 * _nbytes((tm, bw), F32), 6 * _nbytes((tm, bw), BF16), 2 * _nbytes((tm, 3 * LANES), F32)),
        ),
        name="da_prep",
    )(h, h, h, gq, gk, tab)


def _flash(q, k_ref, v_ref, m_ref, l_ref, acc_ref, tk, unroll):
    assert v_ref.shape[-1] == LANES and tk % LANES == 0
    m_ref[...] = jnp.full(m_ref.shape, -jnp.inf, F32)
    l_ref[...] = jnp.zeros(l_ref.shape, F32)
    acc_ref[...] = jnp.zeros(acc_ref.shape, F32)

    def body(c, carry):
        start = pl.multiple_of(c * tk, tk)
        kc = k_ref[pl.ds(start, tk), :]
        vc = v_ref[pl.ds(start, tk), :]
        s = lax.dot_general(q, kc, (((1,), (1,)), ((), ())), preferred_element_type=F32)
        m_prev = m_ref[...]
        m_next = jnp.maximum(m_prev, jnp.max(s, axis=1, keepdims=True))
        p = jnp.exp2(s - pltpu.repeat(m_next, tk // LANES, axis=1))
        alpha = jnp.exp2(m_prev - m_next)
        l_ref[...] = alpha * l_ref[...] + jnp.sum(p, axis=1, keepdims=True)
        acc_ref[...] = alpha * acc_ref[...] + jnp.dot(p.astype(BF16), vc, preferred_element_type=F32)
        m_ref[...] = m_next
        return carry

    lax.fori_loop(0, k_ref.shape[0] // tk, body, 0, unroll=unroll)
    return acc_ref[...] / l_ref[...]


def _attn_scratch(rows):
    return [pltpu.VMEM((rows, LANES), F32)] * 3


def _da_attn_kernel(lam_ref, q_ref, k_ref, v_ref, g_ref, o_ref, m_ref, l_ref, acc_ref, *, tk, unroll, sub, lam_init):
    q = q_ref[...]
    tq = q.shape[0]
    lo = lax.broadcasted_iota(jnp.int32, q.shape, 1) < sub
    zero = jnp.zeros_like(q)
    qs = jnp.concatenate([jnp.where(lo, q, zero), jnp.where(lo, zero, q)], axis=0)
    o = _flash(qs, k_ref, v_ref, m_ref, l_ref, acc_ref, tk, unroll)
    lp = lam_ref[...]
    lam = (jnp.exp(jnp.sum(lp[0:1] * lp[1:2], axis=-1, keepdims=True))
           - jnp.exp(jnp.sum(lp[2:3] * lp[3:4], axis=-1, keepdims=True)) + lam_init)
    o = o[:tq] - lam * o[tq:]
    o_ref[...] = (_rms(o, g_ref[...]) * (1.0 - lam_init)).astype(o_ref.dtype)


def _da_attn(q, k, v, lam_p, g, l, batch, seq, dm, lam_init, *, tq, tk, unroll):
    m, bw = q.shape
    nq = seq // tq
    return pl.pallas_call(
        functools.partial(_da_attn_kernel, tk=tk, unroll=unroll, sub=dm["da_sub"], lam_init=lam_init),
        grid=(batch, DA_HEADS, nq),
        in_specs=[
            pl.BlockSpec((None, 4, dm["da_sub"]), lambda b, h, i: (l, 0, 0)),
            pl.BlockSpec((tq, LANES), lambda b, h, i: (b * nq + i, h)),
            pl.BlockSpec((seq, LANES), lambda b, h, i: (b, h)),
            pl.BlockSpec((seq, LANES), lambda b, h, i: (b, h)),
            pl.BlockSpec((None, 1, LANES), lambda b, h, i: (l, 0, 0)),
        ],
        out_specs=pl.BlockSpec((tq, LANES), lambda b, h, i: (b * nq + i, h)),
        out_shape=jax.ShapeDtypeStruct((m, bw), BF16),
        scratch_shapes=_attn_scratch(2 * tq),
        compiler_params=pltpu.CompilerParams(
            dimension_semantics=("parallel", "parallel", "arbitrary"),
            vmem_limit_bytes=_vmem(4 * _nbytes((seq, LANES), BF16), 4 * _nbytes((tq, LANES), BF16),
                                   3 * _nbytes((2 * tq, LANES), F32), 3 * unroll * _nbytes((2 * tq, tk), F32)),
        ),
        name="da_attn",
    )(lam_p, q, k, v, g)


def _mla_attn_kernel(q_ref, k_ref, v_ref, o_ref, m_ref, l_ref, acc_ref, *, tk, unroll):
    o_ref[...] = _flash(q_ref[...], k_ref, v_ref, m_ref, l_ref, acc_ref, tk, unroll).astype(o_ref.dtype)


def _mla_attn(q, k, v, batch, seq, *, tq, tk, unroll):
    m = q.shape[0]
    dk = q.shape[1] // MLA_HEADS
    nq = seq // tq
    return pl.pallas_call(
        functools.partial(_mla_attn_kernel, tk=tk, unroll=unroll),
        grid=(batch, MLA_HEADS, nq),
        in_specs=[
            pl.BlockSpec((tq, dk), lambda b, h, i: (b * nq + i, h)),
            pl.BlockSpec((seq, dk), lambda b, h, i: (b, h)),
            pl.BlockSpec((seq, LANES), lambda b, h, i: (b, h)),
        ],
        out_specs=pl.BlockSpec((tq, LANES), lambda b, h, i: (b * nq + i, h)),
        out_shape=jax.ShapeDtypeStruct((m, MLA_HEADS * LANES), BF16),
        scratch_shapes=_attn_scratch(tq),
        compiler_params=pltpu.CompilerParams(
            dimension_semantics=("parallel", "parallel", "arbitrary"),
            vmem_limit_bytes=_vmem(4 * _nbytes((seq, dk + LANES), BF16), 4 * _nbytes((tq, dk), BF16),
                                   3 * _nbytes((tq, LANES), F32), 3 * unroll * _nbytes((tq, tk), F32)),
        ),
        name="mla_attn",
    )(q, k, v)


def _mla_q_kernel(cq_ref, ga_ref, w_ref, gq_ref, tab_ref, o_ref, *, qk, rope, scale):
    xn = _rms(cq_ref[...], ga_ref[...]).astype(BF16)
    q = jnp.dot(xn, w_ref[...], preferred_element_type=F32)
    tab = tab_ref[...]
    hw = 2 * LANES
    for h in range(MLA_HEADS):
        blk = q[:, h * hw:(h + 1) * hw]
        sc = lax.rsqrt(jnp.sum(blk * blk, axis=-1, keepdims=True) * (1.0 / qk) + EPS)
        y = blk * sc * gq_ref[:, h * hw:(h + 1) * hw]
        o_ref[:, h * hw:h * hw + LANES] = (_rope(y[:, :LANES], tab, rope // 2) * scale).astype(o_ref.dtype)
        o_ref[:, h * hw + LANES:(h + 1) * hw] = (y[:, LANES:] * scale).astype(o_ref.dtype)


def _mla_q(h, off, ga, w, gq, tab, l, dm, *, tm):
    m = h.shape[0]
    r = dm["q_lora"]
    n = w.shape[-1]
    return pl.pallas_call(
        functools.partial(_mla_q_kernel, qk=dm["mla_qk"], rope=dm["rope"], scale=dm["mla_qk"] ** -0.5 * LOG2_E),
        grid=(m // tm,),
        in_specs=[
            pl.BlockSpec((tm, r), lambda i: (i, off["c_qa"] // r)),
            pl.BlockSpec((None, 1, r), lambda i: (l, 0, 0)),
            pl.BlockSpec((None, r, n), lambda i: (l, 0, 0)),
            pl.BlockSpec((None, 1, n), lambda i: (l, 0, 0)),
            pl.BlockSpec((tm, 3 * LANES), lambda i: (i, 0)),
        ],
        out_specs=pl.BlockSpec((tm, n), lambda i: (i, 0)),
        out_shape=jax.ShapeDtypeStruct((m, n), BF16),
        compiler_params=pltpu.CompilerParams(
            dimension_semantics=("parallel",),
            vmem_limit_bytes=_vmem(2 * _nbytes((tm, r), F32), 2 * _nbytes((r, n), BF16), 2 * _nbytes((tm, n), BF16),
                                   3 * _nbytes((tm, n), F32)),
        ),
        name="mla_q",
    )(h, ga, w, gq, tab)


def _mla_kv_kernel(ckv_ref, kpe_ref, ga_ref, w_ref, gpe_ref, gn_ref, tab_ref, k_ref, v_ref, *, qk, rope):
    xn = _rms(ckv_ref[...], ga_ref[...]).astype(BF16)
    kv = jnp.dot(xn, w_ref[...], preferred_element_type=F32)
    nv = MLA_HEADS * LANES
    v_ref[...] = kv[:, nv:].astype(v_ref.dtype)
    kpe = kpe_ref[...]
    ss_pe = jnp.sum(kpe * kpe, axis=-1, keepdims=True)
    pe = _rope(kpe * gpe_ref[...], tab_ref[...], rope // 2)
    hw = 2 * LANES
    for h in range(MLA_HEADS):
        kn = kv[:, h * LANES:(h + 1) * LANES]
        sc = lax.rsqrt((ss_pe + jnp.sum(kn * kn, axis=-1, keepdims=True)) * (1.0 / qk) + EPS)
        k_ref[:, h * hw:h * hw + LANES] = (pe * sc).astype(k_ref.dtype)
        k_ref[:, h * hw + LANES:(h + 1) * hw] = (kn * gn_ref[:, h * LANES:(h + 1) * LANES] * sc).astype(k_ref.dtype)


def _mla_kv(h, off, ga, w, gpe, gn, tab, l, dm, *, tm):
    m = h.shape[0]
    r = dm["kv_lora"]
    n = w.shape[-1]
    nv = MLA_HEADS * LANES
    return pl.pallas_call(
        functools.partial(_mla_kv_kernel, qk=dm["mla_qk"], rope=dm["rope"]),
        grid=(m // tm,),
        in_specs=[
            pl.BlockSpec((tm, r), lambda i: (i, off["c_kva"] // r)),
            pl.BlockSpec((tm, LANES), lambda i: (i, off["c_kpe"] // LANES)),
            pl.BlockSpec((None, 1, r), lambda i: (l, 0, 0)),
            pl.BlockSpec((None, r, n), lambda i: (l, 0, 0)),
            pl.BlockSpec((None, 1, LANES), lambda i: (l, 0, 0)),
            pl.BlockSpec((None, 1, nv), lambda i: (l, 0, 0)),
            pl.BlockSpec((tm, 3 * LANES), lambda i: (i, 0)),
        ],
        out_specs=[pl.BlockSpec((tm, 2 * nv), lambda i: (i, 0)), pl.BlockSpec((tm, nv), lambda i: (i, 0))],
        out_shape=[jax.ShapeDtypeStruct((m, 2 * nv), BF16), jax.ShapeDtypeStruct((m, nv), BF16)],
        compiler_params=pltpu.CompilerParams(
            dimension_semantics=("parallel",),
            vmem_limit_bytes=_vmem(2 * _nbytes((tm, r), F32), 2 * _nbytes((r, n), BF16), 2 * _nbytes((tm, 3 * nv), BF16),
                                   3 * _nbytes((tm, n), F32)),
        ),
        name="mla_kv",
    )(h, h, ga, w, gpe, gn, tab)


def _lru_kernel(u_ref, up_ref, un_ref, gt_ref, cw_ref, cb_ref, gw_ref, gb_ref, lam_ref, o_ref,
                ue_ref, a_ref, b_ref, h_ref, hb_ref, *, nt, ts):
    t = pl.program_id(1)
    backward = t < nt
    it = jnp.where(backward, nt - 1 - t, t - nt)
    width = u_ref.shape[1]
    halo = SUBLANES

    @pl.when((t == 0) | (t == nt))
    def _():
        h_ref[...] = jnp.zeros_like(h_ref)

    ue_ref[0:halo, :] = jnp.where(it > 0, up_ref[...], 0.0)
    ue_ref[halo:halo + ts, :] = u_ref[...]
    ue_ref[halo + ts:2 * halo + ts, :] = jnp.where(it < nt - 1, un_ref[...], 0.0)
    cw = cw_ref[...]
    pad_l = CONV_W // 2
    uc = cb_ref[...] + sum(cw[j:j + 1] * ue_ref[halo - pad_l + j:halo - pad_l + j + ts, :] for j in range(CONV_W))

    ucb = uc.astype(BF16)
    pre_r, pre_i = [], []
    for n in range(LRU_BLOCKS):
        g2 = jnp.dot(ucb[:, n * LANES:(n + 1) * LANES], gw_ref[n], preferred_element_type=F32)
        pre_r.append(g2[:, :LANES])
        pre_i.append(g2[:, LANES:])
    r = jax.nn.sigmoid(jnp.concatenate(pre_r, axis=1) + gb_ref[0:1, :])
    i = jax.nn.sigmoid(jnp.concatenate(pre_i, axis=1) + gb_ref[1:2, :])
    log_a = (-LRU_C * r) * jax.nn.softplus(-lam_ref[...])
    a_ref[...] = jnp.exp(log_a)
    th = jnp.tanh(log_a)
    b_ref[...] = jnp.sqrt(-2.0 * th / (1.0 - th)) * i * uc

    row = lax.broadcasted_iota(jnp.int32, (SUBLANES, width), 0)
    ng = ts // SUBLANES

    def scan_group(g, h, reverse):
        rows = pl.ds(pl.multiple_of(g * SUBLANES, SUBLANES), SUBLANES)
        a, b = a_ref[rows, :], b_ref[rows, :]
        for d in (1, 2, 4):
            shift = SUBLANES - d if reverse else d
            keep = (row < SUBLANES - d) if reverse else (row >= d)
            b = jnp.where(keep, a * pltpu.roll(b, shift, 0) + b, b)
            a = jnp.where(keep, a * pltpu.roll(a, shift, 0), a)
        return rows, a * h + b

    @pl.when(backward)
    def _():
        base = pl.multiple_of(it * ts, ts)

        def body(k, h):
            rows, hs = scan_group(ng - 1 - k, h, True)
            hb_ref[pl.ds(base + rows.start, SUBLANES), :] = hs
            return hs[0:1, :]

        h_ref[...] = lax.fori_loop(0, ng, body, h_ref[...])

    @pl.when(jnp.logical_not(backward))
    def _():
        def body(k, h):
            rows, hs = scan_group(k, h, False)
            b_ref[rows, :] = hs
            return hs[SUBLANES - 1:SUBLANES, :]

        h_ref[...] = lax.fori_loop(0, ng, body, h_ref[...])
        hsum = b_ref[...] + hb_ref[pl.ds(pl.multiple_of(it * ts, ts), ts), :]
        o_ref[...] = (hsum * jax.nn.gelu(gt_ref[...])).astype(o_ref.dtype)


def _lru(h, off, conv_w, conv_b, gate_w, gate_b, lam, l, batch, seq, dm, *, ts):
    m = h.shape[0]
    bw = dm["bw"]
    nt = seq // ts
    hb = ts // SUBLANES
    n_hblk = m // SUBLANES
    ucol, gcol = off["b_x"] // bw, off["b_gate"] // bw

    def tile(t):
        return jnp.where(t < nt, nt - 1 - t, t - nt)

    def direction(t):
        return jnp.where(t < nt, 1, 0)

    return pl.pallas_call(
        functools.partial(_lru_kernel, nt=nt, ts=ts),
        grid=(batch, 2 * nt),
        in_specs=[
            pl.BlockSpec((ts, bw), lambda b, t: (b * nt + tile(t), ucol)),
            pl.BlockSpec((SUBLANES, bw), lambda b, t: (jnp.maximum((b * nt + tile(t)) * hb - 1, 0), ucol)),
            pl.BlockSpec((SUBLANES, bw), lambda b, t: (jnp.minimum((b * nt + tile(t) + 1) * hb, n_hblk - 1), ucol)),
            pl.BlockSpec((ts, bw), lambda b, t: (b * nt + tile(t), gcol)),
            pl.BlockSpec((None, CONV_W, bw), lambda b, t: (l, 0, 0)),
            pl.BlockSpec((None, 1, bw), lambda b, t: (l, 0, 0)),
            pl.BlockSpec((None, None, LRU_BLOCKS, LANES, 2 * LANES), lambda b, t: (l, direction(t), 0, 0, 0)),
            pl.BlockSpec((None, None, 2, bw), lambda b, t: (l, direction(t), 0, 0)),
            pl.BlockSpec((None, None, 1, bw), lambda b, t: (l, direction(t), 0, 0)),
        ],
        out_specs=pl.BlockSpec((ts, bw), lambda b, t: (b * nt + jnp.maximum(t - nt, 0), 0)),
        out_shape=jax.ShapeDtypeStruct((m, bw), BF16),
        scratch_shapes=[
            pltpu.VMEM((ts + 2 * SUBLANES, bw), F32),
            pltpu.VMEM((ts, bw), F32),
            pltpu.VMEM((ts, bw), F32),
            pltpu.VMEM((1, bw), F32),
            pltpu.VMEM((seq, bw), F32),
        ],
        compiler_params=pltpu.CompilerParams(
            dimension_semantics=("parallel", "arbitrary"),
            vmem_limit_bytes=_vmem(_nbytes((seq, bw), F32), 16 * _nbytes((ts, bw), F32)),
        ),
        name="lru",
    )(h, h, h, h, conv_w, conv_b, gate_w, gate_b, lam)


def _rope_table(positions, rot_dim, period):
    half = rot_dim // 2
    inv = ROPE_THETA ** (-jnp.arange(0, rot_dim, 2, dtype=F32) / rot_dim)
    ang = positions.astype(F32)[..., None] * inv
    c, s = jnp.cos(ang), jnp.sin(ang)
    one = jnp.ones(ang.shape[:-1] + (period - rot_dim,), F32)
    zero = jnp.zeros(ang.shape[:-1] + (period - half,), F32)
    parts = [jnp.concatenate([c, c, one], -1),
             jnp.concatenate([-s, zero], -1),
             jnp.concatenate([zero[..., :half], s, zero[..., :period - rot_dim]], -1)]
    tab = jnp.concatenate([jnp.tile(p, LANES // period) for p in parts], -1)
    return tab.reshape(-1, 3 * LANES)


def kernel(x, positions, mixer_norm_g, w_in, da_q_norm_g, da_k_norm_g, da_lambda, da_subln_g, lru_conv_w, lru_conv_b, lru_gate_w, lru_gate_b, lru_L, mla_q_a_norm_g, mla_w_uq, mla_kv_a_norm_g, mla_w_ukv, mla_q_norm_g, mla_k_norm_g, w_gate_up, b_gate, w_branch_out, w_out, ffn_norm_g, w_ffn_gate, w_ffn_up, w_ffn_down):
    batch, seq, d = x.shape
    depth = w_in.shape[0]
    dm = _dims(d)
    off, h_cols = _h_layout(dm)
    bw, rope, nope = dm["bw"], dm["rope"], dm["nope"]
    m = batch * seq

    w_in_b = w_in.astype(BF16)
    seg = {}
    pos = 0
    for name, width in (("a_q", bw), ("a_k", bw), ("a_v", bw), ("b_gate", bw), ("b_x", bw), ("c_qa", dm["q_lora"]),
                        ("c_kva", dm["kv_lora"]), ("c_kpe", rope), ("g_lr", dm["gate_rank"])):
        seg[name] = w_in_b[..., pos:pos + width]
        pos += width
    order = sorted(off, key=off.get)
    pieces, pos = [], 0
    for name in order:
        if off[name] > pos:
            pieces.append(jnp.zeros(w_in.shape[:2] + (off[name] - pos,), BF16))
        pieces.append(seg[name])
        pos = off[name] + seg[name].shape[-1]
    if h_cols > pos:
        pieces.append(jnp.zeros(w_in.shape[:2] + (h_cols - pos,), BF16))
    w_in_p = jnp.concatenate(pieces, axis=-1)

    uq = mla_w_uq.reshape(depth, dm["q_lora"], MLA_HEADS, dm["mla_qk"])
    w_uq_p = jnp.concatenate([uq[..., :rope], jnp.zeros(uq.shape[:3] + (LANES - rope,), uq.dtype), uq[..., rope:]], -1)
    w_uq_p = w_uq_p.reshape(depth, dm["q_lora"], MLA_HEADS * 2 * LANES).astype(BF16)
    ukv = mla_w_ukv.reshape(depth, dm["kv_lora"], MLA_HEADS, 2, nope)
    w_ukv_p = ukv.transpose(0, 1, 3, 2, 4).reshape(depth, dm["kv_lora"], 2 * MLA_HEADS * nope).astype(BF16)
    pad = jnp.zeros((depth, LANES - rope), F32)
    gq_mla = jnp.tile(jnp.concatenate([mla_q_norm_g[:, :rope], pad, mla_q_norm_g[:, rope:]], -1), (1, MLA_HEADS))[:, None]
    gpe_mla = jnp.concatenate([mla_k_norm_g[:, :rope], pad], -1)[:, None]
    gn_mla = jnp.tile(mla_k_norm_g[:, rope:], (1, MLA_HEADS))[:, None]
    gq_da = jnp.tile(da_q_norm_g, (1, 2))[:, None]
    gk_da = jnp.tile(da_k_norm_g, (1, 2))[:, None]
    gate_w_p = lru_gate_w.transpose(0, 1, 3, 4, 2, 5).reshape(depth, 2, LRU_BLOCKS, LANES, 2 * LANES).astype(BF16)
    w_gate_p = w_gate_up.astype(BF16)
    w_br_p = w_branch_out.astype(BF16)
    w_out_p = w_out.astype(BF16)
    w_fg_p, w_fu_p, w_fd_p = w_ffn_gate.astype(BF16), w_ffn_up.astype(BF16), w_ffn_down.astype(BF16)

    tab_da = _rope_table(positions, dm["da_rot"], dm["da_sub"])
    tab_mla = _rope_table(positions, rope, LANES)

    xf = x.reshape(m, d)
    for l in range(depth):
        lam_init = 0.8 - 0.6 * math.exp(-0.3 * l)
        h = _norm_matmul(xf, mixer_norm_g[:, None], w_in_p, l, tm=1024, tn=h_cols // 9)
        q_a, k_a, v_a = _da_prep(h, off, gq_da, gk_da, tab_da, l, dm, tm=512)
        o_a = _da_attn(q_a, k_a, v_a, da_lambda, da_subln_g[:, None], l, batch, seq, dm, lam_init, tq=256, tk=1024,
                       unroll=2)
        o_b = _lru(h, off, lru_conv_w, lru_conv_b[:, None], gate_w_p, lru_gate_b, lru_L[:, :, None], l, batch, seq, dm,
                   ts=256)
        q_c = _mla_q(h, off, mla_q_a_norm_g[:, None], w_uq_p, gq_mla, tab_mla, l, dm, tm=512)
        k_c, v_c = _mla_kv(h, off, mla_kv_a_norm_g[:, None], w_ukv_p, gpe_mla, gn_mla, tab_mla, l, dm, tm=512)
        o_c = _mla_attn(q_c, k_c, v_c, batch, seq, tq=512, tk=1024, unroll=2)
        z = _branch(o_a, o_b, o_c, h, off["g_lr"], w_br_p, w_gate_p, b_gate, l, tm=512, tn=512)
        xf = _matmul_res(z, w_out_p, l, xf, tm=1024, tn=512, name="out_proj")
        a = _ffn_up(xf, ffn_norm_g[:, None], w_fg_p, w_fu_p, l, tm=1024, tn=256)
        xf = _matmul_res(a, w_fd_p, l, xf, tm=512, tn=256, name="ffn_down")
    return xf.reshape(batch, seq, d)
```

```python
import functools
import math

import jax
import jax.numpy as jnp
from jax import lax
from jax.experimental import pallas as pl
from jax.experimental.pallas import tpu as pltpu

F32 = jnp.float32
BF16 = jnp.bfloat16

LANES = 128
SUBLANES = 8
VMEM_LIMIT_V7X = 60000 * 1024

EPS = 1e-6
LOG2_E = math.log2(math.e)
ROPE_THETA = 500000.0
LRU_C = 8.0
CONV_W = 4
N_BRANCH = 3
DA_HEADS = 8
MLA_HEADS = 8
LRU_BLOCKS = 8


def _dims(d_model):
    bw = d_model // 4
    dm = dict(
        bw=bw,
        da_sub=bw // (2 * DA_HEADS),
        da_head=bw // DA_HEADS,
        lru_bw=bw // LRU_BLOCKS,
        nope=128,
        rope=64,
        mla_v=bw // MLA_HEADS,
        q_lora=3 * d_model // 16,
        kv_lora=d_model // 8,
        gate_rank=d_model // 16,
    )
    dm["da_rot"] = dm["da_sub"] // 4
    dm["mla_qk"] = dm["nope"] + dm["rope"]
    assert dm["da_head"] == LANES and dm["mla_v"] == LANES and dm["lru_bw"] == LANES
    return dm


IN_TILE = 256


def _h_layout(dm):
    bw = dm["bw"]
    sizes = (("a_q", bw), ("a_k", bw), ("a_v", bw), ("b_gate", bw), ("b_x", bw), ("c_qa", dm["q_lora"]),
             ("c_kva", dm["kv_lora"]), ("c_kpe", dm["rope"]), ("g_lr", dm["gate_rank"]))
    src, pos = {}, 0
    for name, width in sizes:
        src[name] = (pos, width)
        pos += width
    main_cols = src["c_kpe"][0]
    assert main_cols % IN_TILE == 0 and all(src[n][0] % IN_TILE == 0 and src[n][1] % IN_TILE == 0 for n, _ in sizes[:-2])
    off = {name: src[name][0] for name, _ in sizes[:-2]}
    tail = dict(g_lr=0, c_kpe=dm["gate_rank"])
    tail_cols = dm["gate_rank"] + LANES
    return src, off, main_cols, tail, tail_cols


def _vmem(*nbytes):
    return min(int(sum(nbytes) * 1.25) + (2 << 20), VMEM_LIMIT_V7X)


def _nbytes(shape, dtype):
    return math.prod(shape) * jnp.dtype(dtype).itemsize


def _rms(x, g):
    return x * lax.rsqrt(jnp.mean(x * x, axis=-1, keepdims=True) + EPS) * g


NORM_ROWS = 256


def _norm_rows(x_ref, g_ref, xn_ref):
    @pl.when(pl.program_id(1) == 0)
    def _():
        g = g_ref[...]

        def body(r, carry):
            rows = pl.ds(pl.multiple_of(r * NORM_ROWS, NORM_ROWS), NORM_ROWS)
            xn_ref[rows, :] = _rms(x_ref[rows, :], g).astype(BF16)
            return carry

        lax.fori_loop(0, x_ref.shape[0] // NORM_ROWS, body, 0)


def _bf16(w):
    return w if w.dtype == BF16 else w.astype(BF16)


def _norm_matmul_kernel(x_ref, g_ref, w_ref, o_ref, xn_ref):
    _norm_rows(x_ref, g_ref, xn_ref)
    o_ref[...] = jnp.dot(xn_ref[...], _bf16(w_ref[...]), preferred_element_type=F32)


def _norm_matmul(x, g, w, l, n, *, tm, tn):
    m, k = x.shape
    return pl.pallas_call(
        _norm_matmul_kernel,
        grid=(m // tm, n // tn),
        in_specs=[
            pl.BlockSpec((tm, k), lambda i, j: (i, 0), pipeline_mode=pl.Buffered(1)),
            pl.BlockSpec((None, 1, k), lambda i, j: (l, 0, 0)),
            pl.BlockSpec((None, k, tn), lambda i, j: (l, 0, j)),
        ],
        out_specs=[pl.BlockSpec((tm, tn), lambda i, j: (i, j)), pl.BlockSpec((tm, k), lambda i, j: (i, 0))],
        out_shape=[jax.ShapeDtypeStruct((m, n), F32), jax.ShapeDtypeStruct((m, k), BF16)],
        compiler_params=pltpu.CompilerParams(
            dimension_semantics=("parallel", "arbitrary"),
            vmem_limit_bytes=_vmem(_nbytes((tm, k), F32), 2 * _nbytes((k, tn), w.dtype), _nbytes((k, tn), BF16),
                                   2 * _nbytes((tm, tn), F32), 2 * _nbytes((tm, k), BF16), 3 * _nbytes((NORM_ROWS, k), F32)),
        ),
        name="in_proj",
    )(x, g, w)


def _matmul_kernel(a_ref, w_ref, o_ref):
    o_ref[...] = jnp.dot(a_ref[...], _bf16(w_ref[...]), preferred_element_type=F32)


def _matmul(a, w, l, *, tm, name):
    m, k = a.shape
    n = w.shape[-1]
    return pl.pallas_call(
        _matmul_kernel,
        grid=(m // tm,),
        in_specs=[pl.BlockSpec((tm, k), lambda i: (i, 0)), pl.BlockSpec((None, k, n), lambda i: (l, 0, 0))],
        out_specs=pl.BlockSpec((tm, n), lambda i: (i, 0)),
        out_shape=jax.ShapeDtypeStruct((m, n), F32),
        compiler_params=pltpu.CompilerParams(
            dimension_semantics=("parallel",),
            vmem_limit_bytes=_vmem(2 * _nbytes((tm, k), a.dtype), 2 * _nbytes((k, n), w.dtype), 3 * _nbytes((tm, n), F32)),
        ),
        name=name,
    )(a, w)


def _ffn_up_kernel(x_ref, g_ref, wg_ref, wu_ref, o_ref, xn_ref):
    _norm_rows(x_ref, g_ref, xn_ref)
    xn = xn_ref[...]
    a = jnp.dot(xn, _bf16(wg_ref[...]), preferred_element_type=F32)
    b = jnp.dot(xn, _bf16(wu_ref[...]), preferred_element_type=F32)
    o_ref[...] = (a * jax.nn.sigmoid(a) * b).astype(o_ref.dtype)


def _ffn_up(x, g, wg, wu, l, *, tm, tn):
    m, k = x.shape
    n = wg.shape[-1]
    return pl.pallas_call(
        _ffn_up_kernel,
        grid=(m // tm, n // tn),
        in_specs=[
            pl.BlockSpec((tm, k), lambda i, j: (i, 0), pipeline_mode=pl.Buffered(1)),
            pl.BlockSpec((None, 1, k), lambda i, j: (l, 0, 0)),
            pl.BlockSpec((None, k, tn), lambda i, j: (l, 0, j)),
            pl.BlockSpec((None, k, tn), lambda i, j: (l, 0, j)),
        ],
        out_specs=pl.BlockSpec((tm, tn), lambda i, j: (i, j)),
        out_shape=jax.ShapeDtypeStruct((m, n), BF16),
        scratch_shapes=[pltpu.VMEM((tm, k), BF16)],
        compiler_params=pltpu.CompilerParams(
            dimension_semantics=("parallel", "arbitrary"),
            vmem_limit_bytes=_vmem(_nbytes((tm, k), F32), 4 * _nbytes((k, tn), wg.dtype), 2 * _nbytes((k, tn), BF16),
                                   2 * _nbytes((tm, tn), BF16), _nbytes((tm, k), BF16), 3 * _nbytes((NORM_ROWS, k), F32),
                                   4 * _nbytes((tm, tn), F32)),
        ),
        name="ffn_up",
    )(x, g, wg, wu)


def _matmul_res_kernel(a_ref, w_ref, r_ref, o_ref):
    o_ref[...] = r_ref[...] + jnp.dot(a_ref[...], _bf16(w_ref[...]), preferred_element_type=F32)


def _matmul_res(a, w, l, res, *, tm, tn, name):
    m, k = a.shape
    n = w.shape[-1]
    return pl.pallas_call(
        _matmul_res_kernel,
        grid=(m // tm, n // tn),
        in_specs=[
            pl.BlockSpec((tm, k), lambda i, j: (i, 0)),
            pl.BlockSpec((None, k, tn), lambda i, j: (l, 0, j)),
            pl.BlockSpec((tm, tn), lambda i, j: (i, j)),
        ],
        out_specs=pl.BlockSpec((tm, tn), lambda i, j: (i, j)),
        out_shape=jax.ShapeDtypeStruct((m, n), F32),
        input_output_aliases={2: 0},
        compiler_params=pltpu.CompilerParams(
            dimension_semantics=("parallel", "parallel"),
            vmem_limit_bytes=_vmem(2 * _nbytes((tm, k), BF16), 2 * _nbytes((k, tn), w.dtype), _nbytes((k, tn), BF16),
                                   5 * _nbytes((tm, tn), F32)),
        ),
        name=name,
    )(a, w, res)


def _branch_kernel(oa_ref, ob_ref, oc_ref, glr_ref, wbr_ref, wg0_ref, wg1_ref, wg2_ref, bg_ref, z_ref):
    glr = glr_ref[...].astype(BF16)
    acc = None
    for n, (o_ref, wg_ref) in enumerate(((oa_ref, wg0_ref), (ob_ref, wg1_ref), (oc_ref, wg2_ref))):
        y = jnp.dot(o_ref[...], _bf16(wbr_ref[n]), preferred_element_type=F32)
        gate = jax.nn.sigmoid(jnp.dot(glr, _bf16(wg_ref[...]), preferred_element_type=F32) + bg_ref[n:n + 1, :])
        acc = gate * y if acc is None else acc + gate * y
    z_ref[...] = acc.astype(z_ref.dtype)


def _branch(o_a, o_b, o_c, h_tail, glr_off, w_br, w_gate, b_gate, l, *, tm, tn):
    m, bw = o_a.shape
    d = w_br.shape[-1]
    rank = w_gate.shape[1]
    nj = d // tn
    o_spec = pl.BlockSpec((tm, bw), lambda i, j: (i, 0))
    wg_specs = [pl.BlockSpec((None, rank, tn), functools.partial(lambda i, j, n: (l, 0, n * nj + j), n=n))
                for n in range(N_BRANCH)]
    return pl.pallas_call(
        _branch_kernel,
        grid=(m // tm, nj),
        in_specs=[
            o_spec, o_spec, o_spec,
            pl.BlockSpec((tm, rank), lambda i, j: (i, glr_off // rank)),
            pl.BlockSpec((None, N_BRANCH, bw, tn), lambda i, j: (l, 0, 0, j)),
            *wg_specs,
            pl.BlockSpec((None, N_BRANCH, tn), lambda i, j: (l, 0, j)),
        ],
        out_specs=pl.BlockSpec((tm, tn), lambda i, j: (i, j)),
        out_shape=jax.ShapeDtypeStruct((m, d), BF16),
        compiler_params=pltpu.CompilerParams(
            dimension_semantics=("parallel", "parallel"),
            vmem_limit_bytes=_vmem(6 * _nbytes((tm, bw), BF16), 2 * _nbytes((tm, rank), F32),
                                   2 * _nbytes((N_BRANCH, bw + rank, tn), w_br.dtype), _nbytes((bw + rank, tn), BF16),
                                   2 * _nbytes((tm, tn), BF16), 5 * _nbytes((tm, tn), F32)),
        ),
        name="branch",
    )(o_a, o_b, o_c, h_tail, w_br, w_gate, w_gate, w_gate, b_gate)


def _rope(y, tab, shift):
    c, s1, s2 = tab[:, 0:LANES], tab[:, LANES:2 * LANES], tab[:, 2 * LANES:3 * LANES]
    return y * c + pltpu.roll(y, LANES - shift, 1) * s1 + pltpu.roll(y, shift, 1) * s2


def _da_prep_kernel(q_ref, k_ref, v_ref, gq_ref, gk_ref, tab_ref, qo_ref, ko_ref, vo_ref, *, sub, rot, scale):
    tab = tab_ref[...]
    lo = lax.broadcasted_iota(jnp.int32, (q_ref.shape[0], LANES), 1) < sub

    def prep(x_ref, g_ref, o_ref, out_scale):
        for h in range(DA_HEADS):
            cols = slice(h * LANES, (h + 1) * LANES)
            x = x_ref[:, cols]
            sq = x * x
            ss0 = jnp.sum(jnp.where(lo, sq, 0.0), axis=-1, keepdims=True)
            ss1 = jnp.sum(jnp.where(lo, 0.0, sq), axis=-1, keepdims=True)
            y = x * lax.rsqrt(jnp.where(lo, ss0, ss1) * (1.0 / sub) + EPS) * g_ref[...]
            o_ref[:, cols] = (_rope(y, tab, rot // 2) * out_scale).astype(o_ref.dtype)

    prep(q_ref, gq_ref, qo_ref, scale)
    prep(k_ref, gk_ref, ko_ref, 1.0)
    vo_ref[...] = v_ref[...].astype(vo_ref.dtype)


def _da_prep(h, off, gq, gk, tab, l, dm, *, tm):
    m = h.shape[0]
    bw = dm["bw"]
    col = lambda name: pl.BlockSpec((tm, bw), functools.partial(lambda i, c: (i, c), c=off[name] // bw))
    g_spec = pl.BlockSpec((None, 1, LANES), lambda i: (l, 0, 0))
    out = jax.ShapeDtypeStruct((m, bw), BF16)
    o_spec = pl.BlockSpec((tm, bw), lambda i: (i, 0))
    return pl.pallas_call(
        functools.partial(_da_prep_kernel, sub=dm["da_sub"], rot=dm["da_rot"], scale=dm["da_sub"] ** -0.5 * LOG2_E),
        grid=(m // tm,),
        in_specs=[col("a_q"), col("a_k"), col("a_v"), g_spec, g_spec,
                  pl.BlockSpec((tm, 3 * LANES), lambda i: (i, 0))],
        out_specs=[o_spec, o_spec, o_spec],
        out_shape=[out, out, out],
        compiler_params=pltpu.CompilerParams(
            dimension_semantics=("parallel",),
            vmem_limit_bytes=_vmem(6 * _nbytes((tm, bw), F32), 6 * _nbytes((tm, bw), BF16), 2 * _nbytes((tm, 3 * LANES), F32)),
        ),
        name="da_prep",
    )(h, h, h, gq, gk, tab)


def _flash(q, k_ref, v_ref, m_ref, l_ref, acc_ref, tk, unroll):
    assert v_ref.shape[-1] == LANES and tk % LANES == 0
    m_ref[...] = jnp.full(m_ref.shape, -jnp.inf, F32)
    l_ref[...] = jnp.zeros(l_ref.shape, F32)
    acc_ref[...] = jnp.zeros(acc_ref.shape, F32)

    def body(c, carry):
        start = pl.multiple_of(c * tk, tk)
        kc = k_ref[pl.ds(start, tk), :]
        vc = v_ref[pl.ds(start, tk), :]
        s = lax.dot_general(q, kc, (((1,), (1,)), ((), ())), preferred_element_type=F32)
        m_prev = m_ref[...]
        m_next = jnp.maximum(m_prev, jnp.max(s, axis=1, keepdims=True))
        p = jnp.exp2(s - jnp.concatenate([m_next] * (tk // LANES), axis=1))
        alpha = jnp.exp2(m_prev - m_next)
        l_ref[...] = alpha * l_ref[...] + jnp.sum(p, axis=1, keepdims=True)
        acc_ref[...] = alpha * acc_ref[...] + jnp.dot(p.astype(BF16), vc, preferred_element_type=F32)
        m_ref[...] = m_next
        return carry

    lax.fori_loop(0, k_ref.shape[0] // tk, body, 0, unroll=unroll)
    return acc_ref[...] / l_ref[...]


def _attn_scratch(rows):
    return [pltpu.VMEM((rows, LANES), F32)] * 3


def _da_attn_kernel(lam_ref, q_ref, k_ref, v_ref, g_ref, o_ref, m_ref, l_ref, acc_ref, *, tk, unroll, sub, lam_init):
    q = q_ref[...]
    tq = q.shape[0]
    lo = lax.broadcasted_iota(jnp.int32, q.shape, 1) < sub
    zero = jnp.zeros_like(q)
    qs = jnp.concatenate([jnp.where(lo, q, zero), jnp.where(lo, zero, q)], axis=0)
    o = _flash(qs, k_ref, v_ref, m_ref, l_ref, acc_ref, tk, unroll)
    lp = lam_ref[...]
    lam = (jnp.exp(jnp.sum(lp[0:1] * lp[1:2], axis=-1, keepdims=True))
           - jnp.exp(jnp.sum(lp[2:3] * lp[3:4], axis=-1, keepdims=True)) + lam_init)
    o = o[:tq] - lam * o[tq:]
    o_ref[...] = (_rms(o, g_ref[...]) * (1.0 - lam_init)).astype(o_ref.dtype)


def _da_attn(q, k, v, lam_p, g, l, batch, seq, dm, lam_init, *, tq, tk, unroll):
    m, bw = q.shape
    nq = seq // tq
    return pl.pallas_call(
        functools.partial(_da_attn_kernel, tk=tk, unroll=unroll, sub=dm["da_sub"], lam_init=lam_init),
        grid=(batch, DA_HEADS, nq),
        in_specs=[
            pl.BlockSpec((None, 4, dm["da_sub"]), lambda b, h, i: (l, 0, 0)),
            pl.BlockSpec((tq, LANES), lambda b, h, i: (b * nq + i, h)),
            pl.BlockSpec((seq, LANES), lambda b, h, i: (b, h)),
            pl.BlockSpec((seq, LANES), lambda b, h, i: (b, h)),
            pl.BlockSpec((None, 1, LANES), lambda b, h, i: (l, 0, 0)),
        ],
        out_specs=pl.BlockSpec((tq, LANES), lambda b, h, i: (b * nq + i, h)),
        out_shape=jax.ShapeDtypeStruct((m, bw), BF16),
        scratch_shapes=_attn_scratch(2 * tq),
        compiler_params=pltpu.CompilerParams(
            dimension_semantics=("parallel", "parallel", "arbitrary"),
            vmem_limit_bytes=_vmem(4 * _nbytes((seq, LANES), BF16), 4 * _nbytes((tq, LANES), BF16),
                                   3 * _nbytes((2 * tq, LANES), F32), 3 * unroll * _nbytes((2 * tq, tk), F32)),
        ),
        name="da_attn",
    )(lam_p, q, k, v, g)


def _mla_attn_kernel(q_ref, k_ref, v_ref, o_ref, m_ref, l_ref, acc_ref, *, tk, unroll):
    o_ref[...] = _flash(q_ref[...], k_ref, v_ref, m_ref, l_ref, acc_ref, tk, unroll).astype(o_ref.dtype)


def _mla_attn(q, k, v, batch, seq, *, tq, tk, unroll):
    m = q.shape[0]
    dk = q.shape[1] // MLA_HEADS
    nq = seq // tq
    return pl.pallas_call(
        functools.partial(_mla_attn_kernel, tk=tk, unroll=unroll),
        grid=(batch, MLA_HEADS, nq),
        in_specs=[
            pl.BlockSpec((tq, dk), lambda b, h, i: (b * nq + i, h)),
            pl.BlockSpec((seq, dk), lambda b, h, i: (b, h)),
            pl.BlockSpec((seq, LANES), lambda b, h, i: (b, h)),
        ],
        out_specs=pl.BlockSpec((tq, LANES), lambda b, h, i: (b * nq + i, h)),
        out_shape=jax.ShapeDtypeStruct((m, MLA_HEADS * LANES), BF16),
        scratch_shapes=_attn_scratch(tq),
        compiler_params=pltpu.CompilerParams(
            dimension_semantics=("parallel", "parallel", "arbitrary"),
            vmem_limit_bytes=_vmem(4 * _nbytes((seq, dk + LANES), BF16), 4 * _nbytes((tq, dk), BF16),
                                   3 * _nbytes((tq, LANES), F32), 3 * unroll * _nbytes((tq, tk), F32)),
        ),
        name="mla_attn",
    )(q, k, v)


def _col_blocks(tm, start, width):
    assert start % IN_TILE == 0 and width % IN_TILE == 0
    return [pl.BlockSpec((tm, IN_TILE), functools.partial(lambda i, c: (i, c), c=start // IN_TILE + t))
            for t in range(width // IN_TILE)]


def _mla_q_kernel(*refs, n_in, qk, rope, scale):
    cq_refs, (ga_ref, w_ref, gq_ref, tab_ref, o_ref) = refs[:n_in], refs[n_in:]
    xn = _rms(jnp.concatenate([r[...] for r in cq_refs], axis=1), ga_ref[...]).astype(BF16)
    q = jnp.dot(xn, w_ref[...], preferred_element_type=F32)
    tab = tab_ref[...]
    hw = 2 * LANES
    for h in range(MLA_HEADS):
        blk = q[:, h * hw:(h + 1) * hw]
        sc = lax.rsqrt(jnp.sum(blk * blk, axis=-1, keepdims=True) * (1.0 / qk) + EPS)
        y = blk * sc * gq_ref[:, h * hw:(h + 1) * hw]
        o_ref[:, h * hw:h * hw + LANES] = (_rope(y[:, :LANES], tab, rope // 2) * scale).astype(o_ref.dtype)
        o_ref[:, h * hw + LANES:(h + 1) * hw] = (y[:, LANES:] * scale).astype(o_ref.dtype)


def _mla_q(h, off, ga, w, gq, tab, l, dm, *, tm):
    m = h.shape[0]
    r = dm["q_lora"]
    n = w.shape[-1]
    return pl.pallas_call(
        functools.partial(_mla_q_kernel, n_in=r // IN_TILE, qk=dm["mla_qk"], rope=dm["rope"],
                          scale=dm["mla_qk"] ** -0.5 * LOG2_E),
        grid=(m // tm,),
        in_specs=[
            *_col_blocks(tm, off["c_qa"], r),
            pl.BlockSpec((None, 1, r), lambda i: (l, 0, 0)),
            pl.BlockSpec((None, r, n), lambda i: (l, 0, 0)),
            pl.BlockSpec((None, 1, n), lambda i: (l, 0, 0)),
            pl.BlockSpec((tm, 3 * LANES), lambda i: (i, 0)),
        ],
        out_specs=pl.BlockSpec((tm, n), lambda i: (i, 0)),
        out_shape=jax.ShapeDtypeStruct((m, n), BF16),
        compiler_params=pltpu.CompilerParams(
            dimension_semantics=("parallel",),
            vmem_limit_bytes=_vmem(2 * _nbytes((tm, r), F32), 2 * _nbytes((r, n), BF16), 2 * _nbytes((tm, n), BF16),
                                   3 * _nbytes((tm, n), F32)),
        ),
        name="mla_q",
    )(*[h] * (r // IN_TILE), ga, w, gq, tab)


def _mla_kv_kernel(*refs, n_in, qk, rope):
    ckv_refs, (kpe_ref, ga_ref, w_ref, gpe_ref, gn_ref, tab_ref, k_ref, v_ref) = refs[:n_in], refs[n_in:]
    xn = _rms(jnp.concatenate([r[...] for r in ckv_refs], axis=1), ga_ref[...]).astype(BF16)
    kv = jnp.dot(xn, w_ref[...], preferred_element_type=F32)
    nv = MLA_HEADS * LANES
    v_ref[...] = kv[:, nv:].astype(v_ref.dtype)
    kpe = kpe_ref[...]
    ss_pe = jnp.sum(kpe * kpe, axis=-1, keepdims=True)
    pe = _rope(kpe * gpe_ref[...], tab_ref[...], rope // 2)
    hw = 2 * LANES
    for h in range(MLA_HEADS):
        kn = kv[:, h * LANES:(h + 1) * LANES]
        sc = lax.rsqrt((ss_pe + jnp.sum(kn * kn, axis=-1, keepdims=True)) * (1.0 / qk) + EPS)
        k_ref[:, h * hw:h * hw + LANES] = (pe * sc).astype(k_ref.dtype)
        k_ref[:, h * hw + LANES:(h + 1) * hw] = (kn * gn_ref[:, h * LANES:(h + 1) * LANES] * sc).astype(k_ref.dtype)


def _mla_kv(h, off, h_tail, tail, ga, w, gpe, gn, tab, l, dm, *, tm):
    m = h.shape[0]
    r = dm["kv_lora"]
    n = w.shape[-1]
    nv = MLA_HEADS * LANES
    return pl.pallas_call(
        functools.partial(_mla_kv_kernel, n_in=r // IN_TILE, qk=dm["mla_qk"], rope=dm["rope"]),
        grid=(m // tm,),
        in_specs=[
            *_col_blocks(tm, off["c_kva"], r),
            pl.BlockSpec((tm, LANES), lambda i: (i, tail["c_kpe"] // LANES)),
            pl.BlockSpec((None, 1, r), lambda i: (l, 0, 0)),
            pl.BlockSpec((None, r, n), lambda i: (l, 0, 0)),
            pl.BlockSpec((None, 1, LANES), lambda i: (l, 0, 0)),
            pl.BlockSpec((None, 1, nv), lambda i: (l, 0, 0)),
            pl.BlockSpec((tm, 3 * LANES), lambda i: (i, 0)),
        ],
        out_specs=[pl.BlockSpec((tm, 2 * nv), lambda i: (i, 0)), pl.BlockSpec((tm, nv), lambda i: (i, 0))],
        out_shape=[jax.ShapeDtypeStruct((m, 2 * nv), BF16), jax.ShapeDtypeStruct((m, nv), BF16)],
        compiler_params=pltpu.CompilerParams(
            dimension_semantics=("parallel",),
            vmem_limit_bytes=_vmem(2 * _nbytes((tm, r), F32), 2 * _nbytes((r, n), BF16), 2 * _nbytes((tm, 3 * nv), BF16),
                                   3 * _nbytes((tm, n), F32)),
        ),
        name="mla_kv",
    )(*[h] * (r // IN_TILE), h_tail, ga, w, gpe, gn, tab)


def _lru_kernel(u_ref, up_ref, un_ref, gt_ref, cw_ref, cb_ref, gw_ref, gb_ref, lam_ref, o_ref,
                ue_ref, a_ref, b_ref, h_ref, hb_ref, *, nt, ts):
    t = pl.program_id(1)
    backward = t < nt
    it = jnp.where(backward, nt - 1 - t, t - nt)
    width = u_ref.shape[1]
    halo = SUBLANES

    @pl.when((t == 0) | (t == nt))
    def _():
        h_ref[...] = jnp.zeros_like(h_ref)

    ue_ref[0:halo, :] = jnp.where(it > 0, up_ref[...], 0.0)
    ue_ref[halo:halo + ts, :] = u_ref[...]
    ue_ref[halo + ts:2 * halo + ts, :] = jnp.where(it < nt - 1, un_ref[...], 0.0)
    cw = cw_ref[...]
    pad_l = CONV_W // 2
    uc = cb_ref[...] + sum(cw[j:j + 1] * ue_ref[halo - pad_l + j:halo - pad_l + j + ts, :] for j in range(CONV_W))

    ucb = uc.astype(BF16)
    pre_r, pre_i = [], []
    for n in range(LRU_BLOCKS):
        g2 = jnp.dot(ucb[:, n * LANES:(n + 1) * LANES], gw_ref[n], preferred_element_type=F32)
        pre_r.append(g2[:, :LANES])
        pre_i.append(g2[:, LANES:])
    r = jax.nn.sigmoid(jnp.concatenate(pre_r, axis=1) + gb_ref[0:1, :])
    i = jax.nn.sigmoid(jnp.concatenate(pre_i, axis=1) + gb_ref[1:2, :])
    log_a = (-LRU_C * r) * jax.nn.softplus(-lam_ref[...])
    a_ref[...] = jnp.exp(log_a)
    th = jnp.tanh(log_a)
    b_ref[...] = jnp.sqrt(-2.0 * th / (1.0 - th)) * i * uc

    row = lax.broadcasted_iota(jnp.int32, (SUBLANES, width), 0)
    ng = ts // SUBLANES

    def scan_group(g, h, reverse):
        rows = pl.ds(pl.multiple_of(g * SUBLANES, SUBLANES), SUBLANES)
        a, b = a_ref[rows, :], b_ref[rows, :]
        for d in (1, 2, 4):
            shift = SUBLANES - d if reverse else d
            keep = (row < SUBLANES - d) if reverse else (row >= d)
            b = jnp.where(keep, a * pltpu.roll(b, shift, 0) + b, b)
            a = jnp.where(keep, a * pltpu.roll(a, shift, 0), a)
        return rows, a * h + b

    @pl.when(backward)
    def _():
        base = pl.multiple_of(it * ts, ts)

        def body(k, h):
            rows, hs = scan_group(ng - 1 - k, h, True)
            hb_ref[pl.ds(base + rows.start, SUBLANES), :] = hs
            return hs[0:1, :]

        h_ref[...] = lax.fori_loop(0, ng, body, h_ref[...])

    @pl.when(jnp.logical_not(backward))
    def _():
        def body(k, h):
            rows, hs = scan_group(k, h, False)
            b_ref[rows, :] = hs
            return hs[SUBLANES - 1:SUBLANES, :]

        h_ref[...] = lax.fori_loop(0, ng, body, h_ref[...])
        hsum = b_ref[...] + hb_ref[pl.ds(pl.multiple_of(it * ts, ts), ts), :]
        o_ref[...] = (hsum * jax.nn.gelu(gt_ref[...])).astype(o_ref.dtype)


def _lru(h, off, conv_w, conv_b, gate_w, gate_b, lam, l, batch, seq, dm, *, ts):
    m = h.shape[0]
    bw = dm["bw"]
    nt = seq // ts
    hb = ts // SUBLANES
    n_hblk = m // SUBLANES
    ucol, gcol = off["b_x"] // bw, off["b_gate"] // bw

    def tile(t):
        return jnp.where(t < nt, nt - 1 - t, t - nt)

    def direction(t):
        return jnp.where(t < nt, 1, 0)

    return pl.pallas_call(
        functools.partial(_lru_kernel, nt=nt, ts=ts),
        grid=(batch, 2 * nt),
        in_specs=[
            pl.BlockSpec((ts, bw), lambda b, t: (b * nt + tile(t), ucol)),
            pl.BlockSpec((SUBLANES, bw), lambda b, t: (jnp.maximum((b * nt + tile(t)) * hb - 1, 0), ucol)),
            pl.BlockSpec((SUBLANES, bw), lambda b, t: (jnp.minimum((b * nt + tile(t) + 1) * hb, n_hblk - 1), ucol)),
            pl.BlockSpec((ts, bw), lambda b, t: (b * nt + tile(t), gcol)),
            pl.BlockSpec((None, CONV_W, bw), lambda b, t: (l, 0, 0)),
            pl.BlockSpec((None, 1, bw), lambda b, t: (l, 0, 0)),
            pl.BlockSpec((None, None, LRU_BLOCKS, LANES, 2 * LANES), lambda b, t: (l, direction(t), 0, 0, 0)),
            pl.BlockSpec((None, None, 2, bw), lambda b, t: (l, direction(t), 0, 0)),
            pl.BlockSpec((None, None, 1, bw), lambda b, t: (l, direction(t), 0, 0)),
        ],
        out_specs=pl.BlockSpec((ts, bw), lambda b, t: (b * nt + jnp.maximum(t - nt, 0), 0)),
        out_shape=jax.ShapeDtypeStruct((m, bw), BF16),
        scratch_shapes=[
            pltpu.VMEM((ts + 2 * SUBLANES, bw), F32),
            pltpu.VMEM((ts, bw), F32),
            pltpu.VMEM((ts, bw), F32),
            pltpu.VMEM((1, bw), F32),
            pltpu.VMEM((seq, bw), F32),
        ],
        compiler_params=pltpu.CompilerParams(
            dimension_semantics=("parallel", "arbitrary"),
            vmem_limit_bytes=_vmem(_nbytes((seq, bw), F32), 16 * _nbytes((ts, bw), F32)),
        ),
        name="lru",
    )(h, h, h, h, conv_w, conv_b, gate_w, gate_b, lam)


def _rope_table(positions, rot_dim, period):
    half = rot_dim // 2
    inv = ROPE_THETA ** (-jnp.arange(0, rot_dim, 2, dtype=F32) / rot_dim)
    ang = positions.astype(F32)[..., None] * inv
    c, s = jnp.cos(ang), jnp.sin(ang)
    one = jnp.ones(ang.shape[:-1] + (period - rot_dim,), F32)
    zero = jnp.zeros(ang.shape[:-1] + (period - half,), F32)
    parts = [jnp.concatenate([c, c, one], -1),
             jnp.concatenate([-s, zero], -1),
             jnp.concatenate([zero[..., :half], s, zero[..., :period - rot_dim]], -1)]
    tab = jnp.concatenate([jnp.tile(p, LANES // period) for p in parts], -1)
    return tab.reshape(-1, 3 * LANES)


def kernel(x, positions, mixer_norm_g, w_in, da_q_norm_g, da_k_norm_g, da_lambda, da_subln_g, lru_conv_w, lru_conv_b, lru_gate_w, lru_gate_b, lru_L, mla_q_a_norm_g, mla_w_uq, mla_kv_a_norm_g, mla_w_ukv, mla_q_norm_g, mla_k_norm_g, w_gate_up, b_gate, w_branch_out, w_out, ffn_norm_g, w_ffn_gate, w_ffn_up, w_ffn_down):
    batch, seq, d = x.shape
    depth = w_in.shape[0]
    dm = _dims(d)
    src, off, main_cols, tail, tail_cols = _h_layout(dm)
    bw, rope, nope = dm["bw"], dm["rope"], dm["nope"]
    m = batch * seq

    def w_in_cols(name):
        start, width = src[name]
        return w_in[..., start:start + width]

    w_tail = jnp.concatenate([w_in_cols("g_lr"), w_in_cols("c_kpe"),
                              jnp.zeros(w_in.shape[:2] + (tail_cols - dm["gate_rank"] - rope,), w_in.dtype)], -1).astype(BF16)

    uq = mla_w_uq.reshape(depth, dm["q_lora"], MLA_HEADS, dm["mla_qk"])
    w_uq_p = jnp.concatenate([uq[..., :rope], jnp.zeros(uq.shape[:3] + (LANES - rope,), uq.dtype), uq[..., rope:]], -1)
    w_uq_p = w_uq_p.reshape(depth, dm["q_lora"], MLA_HEADS * 2 * LANES).astype(BF16)
    ukv = mla_w_ukv.reshape(depth, dm["kv_lora"], MLA_HEADS, 2, nope)
    w_ukv_p = ukv.transpose(0, 1, 3, 2, 4).reshape(depth, dm["kv_lora"], 2 * MLA_HEADS * nope).astype(BF16)
    pad = jnp.zeros((depth, LANES - rope), F32)
    gq_mla = jnp.tile(jnp.concatenate([mla_q_norm_g[:, :rope], pad, mla_q_norm_g[:, rope:]], -1), (1, MLA_HEADS))[:, None]
    gpe_mla = jnp.concatenate([mla_k_norm_g[:, :rope], pad], -1)[:, None]
    gn_mla = jnp.tile(mla_k_norm_g[:, rope:], (1, MLA_HEADS))[:, None]
    gq_da = jnp.tile(da_q_norm_g, (1, 2))[:, None]
    gk_da = jnp.tile(da_k_norm_g, (1, 2))[:, None]
    gate_w_p = lru_gate_w.transpose(0, 1, 3, 4, 2, 5).reshape(depth, 2, LRU_BLOCKS, LANES, 2 * LANES).astype(BF16)
    w_fd_p = w_ffn_down.astype(BF16)

    tab_da = _rope_table(positions, dm["da_rot"], dm["da_sub"])
    tab_mla = _rope_table(positions, rope, LANES)

    xf = x.reshape(m, d)
    for l in range(depth):
        lam_init = 0.8 - 0.6 * math.exp(-0.3 * l)
        h, xn = _norm_matmul(xf, mixer_norm_g[:, None], w_in, l, main_cols, tm=1024, tn=IN_TILE)
        h_tail = _matmul(xn, w_tail, l, tm=1024, name="in_proj_tail")
        q_a, k_a, v_a = _da_prep(h, off, gq_da, gk_da, tab_da, l, dm, tm=512)
        o_a = _da_attn(q_a, k_a, v_a, da_lambda, da_subln_g[:, None], l, batch, seq, dm, lam_init, tq=512, tk=1024,
                       unroll=4)
        o_b = _lru(h, off, lru_conv_w, lru_conv_b[:, None], gate_w_p, lru_gate_b, lru_L[:, :, None], l, batch, seq, dm,
                   ts=256)
        q_c = _mla_q(h, off, mla_q_a_norm_g[:, None], w_uq_p, gq_mla, tab_mla, l, dm, tm=512)
        k_c, v_c = _mla_kv(h, off, h_tail, tail, mla_kv_a_norm_g[:, None], w_ukv_p, gpe_mla, gn_mla, tab_mla, l, dm,
                           tm=512)
        o_c = _mla_attn(q_c, k_c, v_c, batch, seq, tq=1024, tk=1024, unroll=4)
        z = _branch(o_a, o_b, o_c, h_tail, tail["g_lr"], w_branch_out, w_gate_up, b_gate, l, tm=1024, tn=512)
        xf = _matmul_res(z, w_out, l, xf, tm=1024, tn=512, name="out_proj")
        a = _ffn_up(xf, ffn_norm_g[:, None], w_ffn_gate, w_ffn_up, l, tm=1024, tn=256)
        xf = _matmul_res(a, w_fd_p, l, xf, tm=512, tn=256, name="ffn_down")
    return xf.reshape(batch, seq, d)
```

```python
import functools
import math

import jax
import jax.numpy as jnp
from jax import lax
from jax.experimental import pallas as pl
from jax.experimental.pallas import tpu as pltpu

F32 = jnp.float32
BF16 = jnp.bfloat16

LANES = 128
SUBLANES = 8
VMEM_LIMIT_V7X = 60000 * 1024

EPS = 1e-6
LOG2_E = math.log2(math.e)
ROPE_THETA = 500000.0
LRU_C = 8.0
CONV_W = 4
N_BRANCH = 3
DA_HEADS = 8
MLA_HEADS = 8
LRU_BLOCKS = 8


def _dims(d_model):
    bw = d_model // 4
    dm = dict(
        bw=bw,
        da_sub=bw // (2 * DA_HEADS),
        da_head=bw // DA_HEADS,
        lru_bw=bw // LRU_BLOCKS,
        nope=128,
        rope=64,
        mla_v=bw // MLA_HEADS,
        q_lora=3 * d_model // 16,
        kv_lora=d_model // 8,
        gate_rank=d_model // 16,
    )
    dm["da_rot"] = dm["da_sub"] // 4
    dm["mla_qk"] = dm["nope"] + dm["rope"]
    assert dm["da_head"] == LANES and dm["mla_v"] == LANES and dm["lru_bw"] == LANES
    return dm


IN_TILE = 256


def _h_layout(dm):
    bw = dm["bw"]
    sizes = (("a_q", bw), ("a_k", bw), ("a_v", bw), ("b_gate", bw), ("b_x", bw), ("c_qa", dm["q_lora"]),
             ("c_kva", dm["kv_lora"]), ("c_kpe", dm["rope"]), ("g_lr", dm["gate_rank"]))
    src, pos = {}, 0
    for name, width in sizes:
        src[name] = (pos, width)
        pos += width
    main_cols = src["c_kpe"][0]
    assert main_cols % IN_TILE == 0 and all(src[n][0] % IN_TILE == 0 and src[n][1] % IN_TILE == 0 for n, _ in sizes[:-2])
    off = {name: src[name][0] for name, _ in sizes[:-2]}
    tail = dict(g_lr=0, c_kpe=dm["gate_rank"])
    tail_cols = dm["gate_rank"] + LANES
    return src, off, main_cols, tail, tail_cols


def _vmem(*nbytes):
    return min(int(sum(nbytes) * 1.25) + (2 << 20), VMEM_LIMIT_V7X)


def _nbytes(shape, dtype):
    return math.prod(shape) * jnp.dtype(dtype).itemsize


def _rms(x, g):
    return x * lax.rsqrt(jnp.mean(x * x, axis=-1, keepdims=True) + EPS) * g


NORM_ROWS = 256


def _norm_rows(x_ref, g_ref, xn_ref):
    @pl.when(pl.program_id(1) == 0)
    def _():
        g = g_ref[...]

        def body(r, carry):
            rows = pl.ds(pl.multiple_of(r * NORM_ROWS, NORM_ROWS), NORM_ROWS)
            xn_ref[rows, :] = _rms(x_ref[rows, :], g).astype(BF16)
            return carry

        lax.fori_loop(0, x_ref.shape[0] // NORM_ROWS, body, 0)


def _bf16(w):
    return w if w.dtype == BF16 else w.astype(BF16)


def _dot_nt(a, w_t):
    return lax.dot_general(a, _bf16(w_t), (((1,), (1,)), ((), ())), preferred_element_type=F32)


def _norm_matmul_kernel(x_ref, g_ref, w_ref, o_ref, xn_ref):
    _norm_rows(x_ref, g_ref, xn_ref)
    o_ref[...] = _dot_nt(xn_ref[...], w_ref[...])


def _norm_matmul(x, g, w, l, n, *, tm, tn):
    m, k = x.shape
    return pl.pallas_call(
        _norm_matmul_kernel,
        grid=(m // tm, n // tn),
        in_specs=[
            pl.BlockSpec((tm, k), lambda i, j: (i, 0), pipeline_mode=pl.Buffered(1)),
            pl.BlockSpec((None, 1, k), lambda i, j: (l, 0, 0)),
            pl.BlockSpec((None, tn, k), lambda i, j: (l, j, 0)),
        ],
        out_specs=[pl.BlockSpec((tm, tn), lambda i, j: (i, j)), pl.BlockSpec((tm, k), lambda i, j: (i, 0))],
        out_shape=[jax.ShapeDtypeStruct((m, n), F32), jax.ShapeDtypeStruct((m, k), BF16)],
        compiler_params=pltpu.CompilerParams(
            dimension_semantics=("parallel", "arbitrary"),
            vmem_limit_bytes=_vmem(_nbytes((tm, k), F32), 2 * _nbytes((k, tn), w.dtype), _nbytes((k, tn), BF16),
                                   2 * _nbytes((tm, tn), F32), 2 * _nbytes((tm, k), BF16), 3 * _nbytes((NORM_ROWS, k), F32)),
        ),
        name="in_proj",
    )(x, g, w)


def _matmul_kernel(a_ref, w_ref, o_ref):
    o_ref[...] = _dot_nt(a_ref[...], w_ref[...])


def _matmul(a, w, l, *, tm, name):
    m, k = a.shape
    n = w.shape[1]
    return pl.pallas_call(
        _matmul_kernel,
        grid=(m // tm,),
        in_specs=[pl.BlockSpec((tm, k), lambda i: (i, 0)), pl.BlockSpec((None, n, k), lambda i: (l, 0, 0))],
        out_specs=pl.BlockSpec((tm, n), lambda i: (i, 0)),
        out_shape=jax.ShapeDtypeStruct((m, n), F32),
        compiler_params=pltpu.CompilerParams(
            dimension_semantics=("parallel",),
            vmem_limit_bytes=_vmem(2 * _nbytes((tm, k), a.dtype), 2 * _nbytes((k, n), w.dtype), 3 * _nbytes((tm, n), F32)),
        ),
        name=name,
    )(a, w)


def _ffn_up_kernel(x_ref, g_ref, wg_ref, wu_ref, o_ref, xn_ref):
    _norm_rows(x_ref, g_ref, xn_ref)
    xn = xn_ref[...]
    a = jnp.dot(xn, _bf16(wg_ref[...]), preferred_element_type=F32)
    b = jnp.dot(xn, _bf16(wu_ref[...]), preferred_element_type=F32)
    o_ref[...] = (a * jax.nn.sigmoid(a) * b).astype(o_ref.dtype)


def _ffn_up(x, g, wg, wu, l, *, tm, tn):
    m, k = x.shape
    n = wg.shape[-1]
    return pl.pallas_call(
        _ffn_up_kernel,
        grid=(m // tm, n // tn),
        in_specs=[
            pl.BlockSpec((tm, k), lambda i, j: (i, 0), pipeline_mode=pl.Buffered(1)),
            pl.BlockSpec((None, 1, k), lambda i, j: (l, 0, 0)),
            pl.BlockSpec((None, k, tn), lambda i, j: (l, 0, j)),
            pl.BlockSpec((None, k, tn), lambda i, j: (l, 0, j)),
        ],
        out_specs=pl.BlockSpec((tm, tn), lambda i, j: (i, j)),
        out_shape=jax.ShapeDtypeStruct((m, n), BF16),
        scratch_shapes=[pltpu.VMEM((tm, k), BF16)],
        compiler_params=pltpu.CompilerParams(
            dimension_semantics=("parallel", "arbitrary"),
            vmem_limit_bytes=_vmem(_nbytes((tm, k), F32), 4 * _nbytes((k, tn), wg.dtype), 2 * _nbytes((k, tn), BF16),
                                   2 * _nbytes((tm, tn), BF16), _nbytes((tm, k), BF16), 3 * _nbytes((NORM_ROWS, k), F32),
                                   4 * _nbytes((tm, tn), F32)),
        ),
        name="ffn_up",
    )(x, g, wg, wu)


def _matmul_res_kernel(a_ref, w_ref, r_ref, o_ref):
    o_ref[...] = r_ref[...] + jnp.dot(a_ref[...], _bf16(w_ref[...]), preferred_element_type=F32)


def _matmul_res(a, w, l, res, *, tm, tn, name, in_place=True):
    m, k = a.shape
    n = w.shape[-1]
    return pl.pallas_call(
        _matmul_res_kernel,
        grid=(m // tm, n // tn),
        in_specs=[
            pl.BlockSpec((tm, k), lambda i, j: (i, 0)),
            pl.BlockSpec((None, k, tn), lambda i, j: (l, 0, j)),
            pl.BlockSpec((tm, tn), lambda i, j: (i, j)),
        ],
        out_specs=pl.BlockSpec((tm, tn), lambda i, j: (i, j)),
        out_shape=jax.ShapeDtypeStruct((m, n), F32),
        input_output_aliases={2: 0} if in_place else {},
        compiler_params=pltpu.CompilerParams(
            dimension_semantics=("parallel", "parallel"),
            vmem_limit_bytes=_vmem(2 * _nbytes((tm, k), BF16), 2 * _nbytes((k, tn), w.dtype), _nbytes((k, tn), BF16),
                                   5 * _nbytes((tm, tn), F32)),
        ),
        name=name,
    )(a, w, res)


def _branch_kernel(oa_ref, ob_ref, oc_ref, glr_ref, wbr_ref, wg0_ref, wg1_ref, wg2_ref, bg_ref, z_ref):
    glr = glr_ref[...].astype(BF16)
    acc = None
    for n, (o_ref, wg_ref) in enumerate(((oa_ref, wg0_ref), (ob_ref, wg1_ref), (oc_ref, wg2_ref))):
        y = jnp.dot(o_ref[...], _bf16(wbr_ref[n]), preferred_element_type=F32)
        gate = jax.nn.sigmoid(jnp.dot(glr, _bf16(wg_ref[...]), preferred_element_type=F32) + bg_ref[n:n + 1, :])
        acc = gate * y if acc is None else acc + gate * y
    z_ref[...] = acc.astype(z_ref.dtype)


def _branch(o_a, o_b, o_c, h_tail, glr_off, w_br, w_gate, b_gate, l, *, tm, tn):
    m, bw = o_a.shape
    d = w_br.shape[-1]
    rank = w_gate.shape[1]
    nj = d // tn
    o_spec = pl.BlockSpec((tm, bw), lambda i, j: (i, 0))
    wg_specs = [pl.BlockSpec((None, rank, tn), functools.partial(lambda i, j, n: (l, 0, n * nj + j), n=n))
                for n in range(N_BRANCH)]
    return pl.pallas_call(
        _branch_kernel,
        grid=(m // tm, nj),
        in_specs=[
            o_spec, o_spec, o_spec,
            pl.BlockSpec((tm, rank), lambda i, j: (i, glr_off // rank)),
            pl.BlockSpec((None, N_BRANCH, bw, tn), lambda i, j: (l, 0, 0, j)),
            *wg_specs,
            pl.BlockSpec((None, N_BRANCH, tn), lambda i, j: (l, 0, j)),
        ],
        out_specs=pl.BlockSpec((tm, tn), lambda i, j: (i, j)),
        out_shape=jax.ShapeDtypeStruct((m, d), BF16),
        compiler_params=pltpu.CompilerParams(
            dimension_semantics=("parallel", "parallel"),
            vmem_limit_bytes=_vmem(6 * _nbytes((tm, bw), BF16), 2 * _nbytes((tm, rank), F32),
                                   2 * _nbytes((N_BRANCH, bw + rank, tn), w_br.dtype), _nbytes((bw + rank, tn), BF16),
                                   2 * _nbytes((tm, tn), BF16), 5 * _nbytes((tm, tn), F32)),
        ),
        name="branch",
    )(o_a, o_b, o_c, h_tail, w_br, w_gate, w_gate, w_gate, b_gate)


def _rope(y, tab, shift):
    c, s1, s2 = tab[:, 0:LANES], tab[:, LANES:2 * LANES], tab[:, 2 * LANES:3 * LANES]
    return y * c + pltpu.roll(y, LANES - shift, 1) * s1 + pltpu.roll(y, shift, 1) * s2


def _da_prep_kernel(q_ref, k_ref, v_ref, gq_ref, gk_ref, tab_ref, qo_ref, ko_ref, vo_ref, *, sub, rot, scale):
    tab = tab_ref[...]
    lo = lax.broadcasted_iota(jnp.int32, (q_ref.shape[0], LANES), 1) < sub

    def prep(x_ref, g_ref, o_ref, out_scale):
        for h in range(DA_HEADS):
            cols = slice(h * LANES, (h + 1) * LANES)
            x = x_ref[:, cols]
            sq = x * x
            ss0 = jnp.sum(jnp.where(lo, sq, 0.0), axis=-1, keepdims=True)
            ss1 = jnp.sum(jnp.where(lo, 0.0, sq), axis=-1, keepdims=True)
            y = x * lax.rsqrt(jnp.where(lo, ss0, ss1) * (1.0 / sub) + EPS) * g_ref[...]
            o_ref[:, cols] = (_rope(y, tab, rot // 2) * out_scale).astype(o_ref.dtype)

    prep(q_ref, gq_ref, qo_ref, scale)
    prep(k_ref, gk_ref, ko_ref, 1.0)
    _store_transposed(v_ref[...], vo_ref)


def _store_transposed(v, vt_ref):
    for h in range(v.shape[1] // LANES):
        cols = slice(h * LANES, (h + 1) * LANES)
        vt_ref[cols, :] = v[:, cols].T.astype(vt_ref.dtype)


def _vt_spec(tm, width, seq):
    per_seq = seq // tm
    return pl.BlockSpec((width, tm), lambda i: (i // per_seq, i % per_seq))


def _da_prep(h, off, gq, gk, tab, l, batch, seq, dm, *, tm):
    m = h.shape[0]
    bw = dm["bw"]
    col = lambda name: pl.BlockSpec((tm, bw), functools.partial(lambda i, c: (i, c), c=off[name] // bw))
    g_spec = pl.BlockSpec((None, 1, LANES), lambda i: (l, 0, 0))
    out = jax.ShapeDtypeStruct((m, bw), BF16)
    o_spec = pl.BlockSpec((tm, bw), lambda i: (i, 0))
    return pl.pallas_call(
        functools.partial(_da_prep_kernel, sub=dm["da_sub"], rot=dm["da_rot"], scale=dm["da_sub"] ** -0.5 * LOG2_E),
        grid=(m // tm,),
        in_specs=[col("a_q"), col("a_k"), col("a_v"), g_spec, g_spec,
                  pl.BlockSpec((tm, 3 * LANES), lambda i: (i, 0))],
        out_specs=[o_spec, o_spec, _vt_spec(tm, bw, seq)],
        out_shape=[out, out, jax.ShapeDtypeStruct((batch * bw, seq), BF16)],
        compiler_params=pltpu.CompilerParams(
            dimension_semantics=("parallel",),
            vmem_limit_bytes=_vmem(6 * _nbytes((tm, bw), F32), 6 * _nbytes((tm, bw), BF16), 2 * _nbytes((tm, 3 * LANES), F32)),
        ),
        name="da_prep",
    )(h, h, h, gq, gk, tab)


def _flash_t(q, k_ref, vt_ref, tk):
    rows, seq = q.shape[0], k_ref.shape[0]
    m = jnp.full((1, rows), -jnp.inf, F32)
    l = jnp.zeros((1, rows), F32)
    acc = jnp.zeros((vt_ref.shape[0], rows), F32)
    for c in range(seq // tk):
        kc = k_ref[c * tk:(c + 1) * tk, :]
        s = lax.dot_general(kc, q, (((1,), (1,)), ((), ())), preferred_element_type=F32)
        m_next = jnp.maximum(m, jnp.max(s, axis=0, keepdims=True))
        p = jnp.exp2(s - m_next)
        alpha = jnp.exp2(m - m_next)
        l = alpha * l + jnp.sum(p, axis=0, keepdims=True)
        acc = alpha * acc + jnp.dot(vt_ref[:, c * tk:(c + 1) * tk], p.astype(BF16), preferred_element_type=F32)
        m = m_next
    return acc / l


def _da_attn_kernel(lam_ref, q_ref, k_ref, vt_ref, g_ref, o_ref, *, tk, sub, lam_init):
    q = q_ref[...]
    tq = q.shape[0]
    lo = lax.broadcasted_iota(jnp.int32, q.shape, 1) < sub
    zero = jnp.zeros_like(q)
    qs = jnp.concatenate([jnp.where(lo, q, zero), jnp.where(lo, zero, q)], axis=0)
    ot = _flash_t(qs, k_ref, vt_ref, tk)
    lp = lam_ref[...]
    lam = (jnp.exp(jnp.sum(lp[0:1] * lp[1:2], axis=-1, keepdims=True))
           - jnp.exp(jnp.sum(lp[2:3] * lp[3:4], axis=-1, keepdims=True)) + lam_init)
    o = (ot[:, :tq] - lam * ot[:, tq:]).T
    o_ref[...] = (_rms(o, g_ref[...]) * (1.0 - lam_init)).astype(o_ref.dtype)


def _da_attn(q, k, vt, lam_p, g, l, batch, seq, dm, lam_init, *, tq, tk):
    m, bw = q.shape
    nq = seq // tq
    tk = min(tk, seq)
    assert seq % tq == 0 and seq % tk == 0
    return pl.pallas_call(
        functools.partial(_da_attn_kernel, tk=tk, sub=dm["da_sub"], lam_init=lam_init),
        grid=(batch, DA_HEADS, nq),
        in_specs=[
            pl.BlockSpec((None, 4, dm["da_sub"]), lambda b, h, i: (l, 0, 0)),
            pl.BlockSpec((tq, LANES), lambda b, h, i: (b * nq + i, h)),
            pl.BlockSpec((seq, LANES), lambda b, h, i: (b, h)),
            pl.BlockSpec((LANES, seq), lambda b, h, i: (b * DA_HEADS + h, 0)),
            pl.BlockSpec((None, 1, LANES), lambda b, h, i: (l, 0, 0)),
        ],
        out_specs=pl.BlockSpec((tq, LANES), lambda b, h, i: (b * nq + i, h)),
        out_shape=jax.ShapeDtypeStruct((m, bw), BF16),
        compiler_params=pltpu.CompilerParams(
            dimension_semantics=("parallel", "parallel", "arbitrary"),
            vmem_limit_bytes=_vmem(4 * _nbytes((seq, LANES), BF16), 4 * _nbytes((tq, LANES), BF16),
                                   2 * (seq // tk) * _nbytes((2 * tq, tk), F32)),
        ),
        name="da_attn",
    )(lam_p, q, k, vt, g)


def _mla_attn_kernel(q_ref, k_ref, vt_ref, o_ref, *, tk):
    o_ref[...] = _flash_t(q_ref[...], k_ref, vt_ref, tk).T.astype(o_ref.dtype)


def _mla_attn(q, k, vt, batch, seq, *, tq, tk):
    m = q.shape[0]
    dk = q.shape[1] // MLA_HEADS
    nq = seq // tq
    tk = min(tk, seq)
    assert seq % tq == 0 and seq % tk == 0
    return pl.pallas_call(
        functools.partial(_mla_attn_kernel, tk=tk),
        grid=(batch, MLA_HEADS, nq),
        in_specs=[
            pl.BlockSpec((tq, dk), lambda b, h, i: (b * nq + i, h)),
            pl.BlockSpec((seq, dk), lambda b, h, i: (b, h)),
            pl.BlockSpec((LANES, seq), lambda b, h, i: (b * MLA_HEADS + h, 0)),
        ],
        out_specs=pl.BlockSpec((tq, LANES), lambda b, h, i: (b * nq + i, h)),
        out_shape=jax.ShapeDtypeStruct((m, MLA_HEADS * LANES), BF16),
        compiler_params=pltpu.CompilerParams(
            dimension_semantics=("parallel", "parallel", "arbitrary"),
            vmem_limit_bytes=_vmem(4 * _nbytes((seq, dk + LANES), BF16), 4 * _nbytes((tq, dk), BF16),
                                   2 * (seq // tk) * _nbytes((tq, tk), F32)),
        ),
        name="mla_attn",
    )(q, k, vt)


def _col_blocks(tm, start, width):
    assert start % IN_TILE == 0 and width % IN_TILE == 0
    return [pl.BlockSpec((tm, IN_TILE), functools.partial(lambda i, c: (i, c), c=start // IN_TILE + t))
            for t in range(width // IN_TILE)]


def _mla_q_kernel(*refs, n_in, qk, rope, scale):
    cq_refs, (ga_ref, w_ref, gq_ref, tab_ref, o_ref) = refs[:n_in], refs[n_in:]
    xn = _rms(jnp.concatenate([r[...] for r in cq_refs], axis=1), ga_ref[...]).astype(BF16)
    q = jnp.dot(xn, w_ref[...], preferred_element_type=F32)
    tab = tab_ref[...]
    hw = 2 * LANES
    for h in range(MLA_HEADS):
        blk = q[:, h * hw:(h + 1) * hw]
        sc = lax.rsqrt(jnp.sum(blk * blk, axis=-1, keepdims=True) * (1.0 / qk) + EPS)
        y = blk * sc * gq_ref[:, h * hw:(h + 1) * hw]
        o_ref[:, h * hw:h * hw + LANES] = (_rope(y[:, :LANES], tab, rope // 2) * scale).astype(o_ref.dtype)
        o_ref[:, h * hw + LANES:(h + 1) * hw] = (y[:, LANES:] * scale).astype(o_ref.dtype)


def _mla_q(h, off, ga, w, gq, tab, l, dm, *, tm):
    m = h.shape[0]
    r = dm["q_lora"]
    n = w.shape[-1]
    return pl.pallas_call(
        functools.partial(_mla_q_kernel, n_in=r // IN_TILE, qk=dm["mla_qk"], rope=dm["rope"],
                          scale=dm["mla_qk"] ** -0.5 * LOG2_E),
        grid=(m // tm,),
        in_specs=[
            *_col_blocks(tm, off["c_qa"], r),
            pl.BlockSpec((None, 1, r), lambda i: (l, 0, 0)),
            pl.BlockSpec((None, r, n), lambda i: (l, 0, 0)),
            pl.BlockSpec((None, 1, n), lambda i: (l, 0, 0)),
            pl.BlockSpec((tm, 3 * LANES), lambda i: (i, 0)),
        ],
        out_specs=pl.BlockSpec((tm, n), lambda i: (i, 0)),
        out_shape=jax.ShapeDtypeStruct((m, n), BF16),
        compiler_params=pltpu.CompilerParams(
            dimension_semantics=("parallel",),
            vmem_limit_bytes=_vmem(2 * _nbytes((tm, r), F32), 2 * _nbytes((r, n), BF16), 2 * _nbytes((tm, n), BF16),
                                   3 * _nbytes((tm, n), F32)),
        ),
        name="mla_q",
    )(*[h] * (r // IN_TILE), ga, w, gq, tab)


def _mla_kv_kernel(*refs, n_in, qk, rope):
    ckv_refs, (kpe_ref, ga_ref, w_ref, gpe_ref, gn_ref, tab_ref, k_ref, v_ref) = refs[:n_in], refs[n_in:]
    xn = _rms(jnp.concatenate([r[...] for r in ckv_refs], axis=1), ga_ref[...]).astype(BF16)
    kv = jnp.dot(xn, w_ref[...], preferred_element_type=F32)
    nv = MLA_HEADS * LANES
    _store_transposed(kv[:, nv:], v_ref)
    kpe = kpe_ref[...]
    ss_pe = jnp.sum(kpe * kpe, axis=-1, keepdims=True)
    pe = _rope(kpe * gpe_ref[...], tab_ref[...], rope // 2)
    hw = 2 * LANES
    for h in range(MLA_HEADS):
        kn = kv[:, h * LANES:(h + 1) * LANES]
        sc = lax.rsqrt((ss_pe + jnp.sum(kn * kn, axis=-1, keepdims=True)) * (1.0 / qk) + EPS)
        k_ref[:, h * hw:h * hw + LANES] = (pe * sc).astype(k_ref.dtype)
        k_ref[:, h * hw + LANES:(h + 1) * hw] = (kn * gn_ref[:, h * LANES:(h + 1) * LANES] * sc).astype(k_ref.dtype)


def _mla_kv(h, off, h_tail, tail, ga, w, gpe, gn, tab, l, batch, seq, dm, *, tm):
    m = h.shape[0]
    r = dm["kv_lora"]
    n = w.shape[-1]
    nv = MLA_HEADS * LANES
    return pl.pallas_call(
        functools.partial(_mla_kv_kernel, n_in=r // IN_TILE, qk=dm["mla_qk"], rope=dm["rope"]),
        grid=(m // tm,),
        in_specs=[
            *_col_blocks(tm, off["c_kva"], r),
            pl.BlockSpec((tm, LANES), lambda i: (i, tail["c_kpe"] // LANES)),
            pl.BlockSpec((None, 1, r), lambda i: (l, 0, 0)),
            pl.BlockSpec((None, r, n), lambda i: (l, 0, 0)),
            pl.BlockSpec((None, 1, LANES), lambda i: (l, 0, 0)),
            pl.BlockSpec((None, 1, nv), lambda i: (l, 0, 0)),
            pl.BlockSpec((tm, 3 * LANES), lambda i: (i, 0)),
        ],
        out_specs=[pl.BlockSpec((tm, 2 * nv), lambda i: (i, 0)), _vt_spec(tm, nv, seq)],
        out_shape=[jax.ShapeDtypeStruct((m, 2 * nv), BF16), jax.ShapeDtypeStruct((batch * nv, seq), BF16)],
        compiler_params=pltpu.CompilerParams(
            dimension_semantics=("parallel",),
            vmem_limit_bytes=_vmem(2 * _nbytes((tm, r), F32), 2 * _nbytes((r, n), BF16), 2 * _nbytes((tm, 3 * nv), BF16),
                                   3 * _nbytes((tm, n), F32)),
        ),
        name="mla_kv",
    )(*[h] * (r // IN_TILE), h_tail, ga, w, gpe, gn, tab)


def _lru_kernel(u_ref, up_ref, un_ref, gt_ref, cw_ref, cb_ref, gw_ref, gb_ref, lam_ref, o_ref,
                ue_ref, a_ref, b_ref, h_ref, hb_ref, *, nt, ts):
    t = pl.program_id(1)
    backward = t < nt
    it = jnp.where(backward, nt - 1 - t, t - nt)
    width = u_ref.shape[1]
    halo = SUBLANES

    @pl.when((t == 0) | (t == nt))
    def _():
        h_ref[...] = jnp.zeros_like(h_ref)

    ue_ref[0:halo, :] = jnp.where(it > 0, up_ref[...], 0.0)
    ue_ref[halo:halo + ts, :] = u_ref[...]
    ue_ref[halo + ts:2 * halo + ts, :] = jnp.where(it < nt - 1, un_ref[...], 0.0)
    cw = cw_ref[...]
    pad_l = CONV_W // 2
    uc = cb_ref[...] + sum(cw[j:j + 1] * ue_ref[halo - pad_l + j:halo - pad_l + j + ts, :] for j in range(CONV_W))

    ucb = uc.astype(BF16)
    pre_r, pre_i = [], []
    for n in range(LRU_BLOCKS):
        g2 = jnp.dot(ucb[:, n * LANES:(n + 1) * LANES], gw_ref[n], preferred_element_type=F32)
        pre_r.append(g2[:, :LANES])
        pre_i.append(g2[:, LANES:])
    r = jax.nn.sigmoid(jnp.concatenate(pre_r, axis=1) + gb_ref[0:1, :])
    i = jax.nn.sigmoid(jnp.concatenate(pre_i, axis=1) + gb_ref[1:2, :])
    log_a = (-LRU_C * r) * jax.nn.softplus(-lam_ref[...])
    a_ref[...] = jnp.exp(log_a)
    th = jnp.tanh(log_a)
    b_ref[...] = jnp.sqrt(-2.0 * th / (1.0 - th)) * i * uc

    row = lax.broadcasted_iota(jnp.int32, (SUBLANES, width), 0)
    ng = ts // SUBLANES

    def scan_group(g, h, reverse):
        rows = pl.ds(pl.multiple_of(g * SUBLANES, SUBLANES), SUBLANES)
        a, b = a_ref[rows, :], b_ref[rows, :]
        for d in (1, 2, 4):
            shift = SUBLANES - d if reverse else d
            keep = (row < SUBLANES - d) if reverse else (row >= d)
            b = jnp.where(keep, a * pltpu.roll(b, shift, 0) + b, b)
            a = jnp.where(keep, a * pltpu.roll(a, shift, 0), a)
        return rows, a * h + b

    @pl.when(backward)
    def _():
        base = pl.multiple_of(it * ts, ts)

        def body(k, h):
            rows, hs = scan_group(ng - 1 - k, h, True)
            hb_ref[pl.ds(base + rows.start, SUBLANES), :] = hs
            return hs[0:1, :]

        h_ref[...] = lax.fori_loop(0, ng, body, h_ref[...])

    @pl.when(jnp.logical_not(backward))
    def _():
        def body(k, h):
            rows, hs = scan_group(k, h, False)
            b_ref[rows, :] = hs
            return hs[SUBLANES - 1:SUBLANES, :]

        h_ref[...] = lax.fori_loop(0, ng, body, h_ref[...])
        hsum = b_ref[...] + hb_ref[pl.ds(pl.multiple_of(it * ts, ts), ts), :]
        o_ref[...] = (hsum * jax.nn.gelu(gt_ref[...])).astype(o_ref.dtype)


def _lru(h, off, conv_w, conv_b, gate_w, gate_b, lam, l, batch, seq, dm, *, ts):
    m = h.shape[0]
    bw = dm["bw"]
    nt = seq // ts
    hb = ts // SUBLANES
    n_hblk = m // SUBLANES
    ucol, gcol = off["b_x"] // bw, off["b_gate"] // bw

    def tile(t):
        return jnp.where(t < nt, nt - 1 - t, t - nt)

    def direction(t):
        return jnp.where(t < nt, 1, 0)

    return pl.pallas_call(
        functools.partial(_lru_kernel, nt=nt, ts=ts),
        grid=(batch, 2 * nt),
        in_specs=[
            pl.BlockSpec((ts, bw), lambda b, t: (b * nt + tile(t), ucol)),
            pl.BlockSpec((SUBLANES, bw), lambda b, t: (jnp.maximum((b * nt + tile(t)) * hb - 1, 0), ucol)),
            pl.BlockSpec((SUBLANES, bw), lambda b, t: (jnp.minimum((b * nt + tile(t) + 1) * hb, n_hblk - 1), ucol)),
            pl.BlockSpec((ts, bw), lambda b, t: (b * nt + tile(t), gcol)),
            pl.BlockSpec((None, CONV_W, bw), lambda b, t: (l, 0, 0)),
            pl.BlockSpec((None, 1, bw), lambda b, t: (l, 0, 0)),
            pl.BlockSpec((None, None, LRU_BLOCKS, LANES, 2 * LANES), lambda b, t: (l, direction(t), 0, 0, 0)),
            pl.BlockSpec((None, None, 2, bw), lambda b, t: (l, direction(t), 0, 0)),
            pl.BlockSpec((None, None, 1, bw), lambda b, t: (l, direction(t), 0, 0)),
        ],
        out_specs=pl.BlockSpec((ts, bw), lambda b, t: (b * nt + jnp.maximum(t - nt, 0), 0)),
        out_shape=jax.ShapeDtypeStruct((m, bw), BF16),
        scratch_shapes=[
            pltpu.VMEM((ts + 2 * SUBLANES, bw), F32),
            pltpu.VMEM((ts, bw), F32),
            pltpu.VMEM((ts, bw), F32),
            pltpu.VMEM((1, bw), F32),
            pltpu.VMEM((seq, bw), F32),
        ],
        compiler_params=pltpu.CompilerParams(
            dimension_semantics=("parallel", "arbitrary"),
            vmem_limit_bytes=_vmem(_nbytes((seq, bw), F32), 16 * _nbytes((ts, bw), F32)),
        ),
        name="lru",
    )(h, h, h, h, conv_w, conv_b, gate_w, gate_b, lam)


def _rope_table(positions, rot_dim, period):
    half = rot_dim // 2
    inv = ROPE_THETA ** (-jnp.arange(0, rot_dim, 2, dtype=F32) / rot_dim)
    ang = positions.astype(F32)[..., None] * inv
    c, s = jnp.cos(ang), jnp.sin(ang)
    one = jnp.ones(ang.shape[:-1] + (period - rot_dim,), F32)
    zero = jnp.zeros(ang.shape[:-1] + (period - half,), F32)
    parts = [jnp.concatenate([c, c, one], -1),
             jnp.concatenate([-s, zero], -1),
             jnp.concatenate([zero[..., :half], s, zero[..., :period - rot_dim]], -1)]
    tab = jnp.concatenate([jnp.tile(p, LANES // period) for p in parts], -1)
    return tab.reshape(-1, 3 * LANES)


def kernel(x, positions, mixer_norm_g, w_in, da_q_norm_g, da_k_norm_g, da_lambda, da_subln_g, lru_conv_w, lru_conv_b, lru_gate_w, lru_gate_b, lru_L, mla_q_a_norm_g, mla_w_uq, mla_kv_a_norm_g, mla_w_ukv, mla_q_norm_g, mla_k_norm_g, w_gate_up, b_gate, w_branch_out, w_out, ffn_norm_g, w_ffn_gate, w_ffn_up, w_ffn_down):
    batch, seq, d = x.shape
    depth = w_in.shape[0]
    dm = _dims(d)
    src, off, main_cols, tail, tail_cols = _h_layout(dm)
    bw, rope, nope = dm["bw"], dm["rope"], dm["nope"]
    m = batch * seq

    w_in_t = jnp.swapaxes(w_in, 1, 2)

    def w_in_rows(name):
        start, width = src[name]
        return w_in_t[:, start:start + width]

    w_tail = jnp.concatenate([w_in_rows("g_lr"), w_in_rows("c_kpe"),
                              jnp.zeros((depth, tail_cols - dm["gate_rank"] - rope, d), w_in.dtype)], 1).astype(BF16)

    uq = mla_w_uq.reshape(depth, dm["q_lora"], MLA_HEADS, dm["mla_qk"])
    w_uq_p = jnp.concatenate([uq[..., :rope], jnp.zeros(uq.shape[:3] + (LANES - rope,), uq.dtype), uq[..., rope:]], -1)
    w_uq_p = w_uq_p.reshape(depth, dm["q_lora"], MLA_HEADS * 2 * LANES).astype(BF16)
    ukv = mla_w_ukv.reshape(depth, dm["kv_lora"], MLA_HEADS, 2, nope)
    w_ukv_p = ukv.transpose(0, 1, 3, 2, 4).reshape(depth, dm["kv_lora"], 2 * MLA_HEADS * nope).astype(BF16)
    pad = jnp.zeros((depth, LANES - rope), F32)
    gq_mla = jnp.tile(jnp.concatenate([mla_q_norm_g[:, :rope], pad, mla_q_norm_g[:, rope:]], -1), (1, MLA_HEADS))[:, None]
    gpe_mla = jnp.concatenate([mla_k_norm_g[:, :rope], pad], -1)[:, None]
    gn_mla = jnp.tile(mla_k_norm_g[:, rope:], (1, MLA_HEADS))[:, None]
    gq_da = jnp.tile(da_q_norm_g, (1, 2))[:, None]
    gk_da = jnp.tile(da_k_norm_g, (1, 2))[:, None]
    gate_w_p = lru_gate_w.transpose(0, 1, 3, 4, 2, 5).reshape(depth, 2, LRU_BLOCKS, LANES, 2 * LANES).astype(BF16)
    w_fd_p = w_ffn_down.astype(BF16)

    tab_da = _rope_table(positions, dm["da_rot"], dm["da_sub"])
    tab_mla = _rope_table(positions, rope, LANES)

    xf = x.reshape(m, d)
    for l in range(depth):
        lam_init = 0.8 - 0.6 * math.exp(-0.3 * l)
        h, xn = _norm_matmul(xf, mixer_norm_g[:, None], w_in_t, l, main_cols, tm=1024, tn=IN_TILE)
        h_tail = _matmul(xn, w_tail, l, tm=1024, name="in_proj_tail")
        q_a, k_a, vt_a = _da_prep(h, off, gq_da, gk_da, tab_da, l, batch, seq, dm, tm=512)
        o_a = _da_attn(q_a, k_a, vt_a, da_lambda, da_subln_g[:, None], l, batch, seq, dm, lam_init, tq=512, tk=4096)
        o_b = _lru(h, off, lru_conv_w, lru_conv_b[:, None], gate_w_p, lru_gate_b, lru_L[:, :, None], l, batch, seq, dm,
                   ts=256)
        q_c = _mla_q(h, off, mla_q_a_norm_g[:, None], w_uq_p, gq_mla, tab_mla, l, dm, tm=512)
        k_c, vt_c = _mla_kv(h, off, h_tail, tail, mla_kv_a_norm_g[:, None], w_ukv_p, gpe_mla, gn_mla, tab_mla, l, batch,
                            seq, dm, tm=512)
        o_c = _mla_attn(q_c, k_c, vt_c, batch, seq, tq=1024, tk=4096)
        z = _branch(o_a, o_b, o_c, h_tail, tail["g_lr"], w_branch_out, w_gate_up, b_gate, l, tm=1024, tn=512)
        xf = _matmul_res(z, w_out, l, xf, tm=1024, tn=512, name="out_proj", in_place=l > 0)
        a = _ffn_up(xf, ffn_norm_g[:, None], w_ffn_gate, w_ffn_up, l, tm=1024, tn=256)
        xf = _matmul_res(a, w_fd_p, l, xf, tm=512, tn=256, name="ffn_down")
    return xf.reshape(batch, seq, d)
```

```python
import functools
import math

import jax
import jax.numpy as jnp
from jax import lax
from jax.experimental import pallas as pl
from jax.experimental.pallas import tpu as pltpu

F32 = jnp.float32
BF16 = jnp.bfloat16

LANES = 128
SUBLANES = 8
VMEM_LIMIT_V7X = 60000 * 1024

EPS = 1e-6
LOG2_E = math.log2(math.e)
ROPE_THETA = 500000.0
LRU_C = 8.0
CONV_W = 4
N_BRANCH = 3
DA_HEADS = 8
MLA_HEADS = 8
LRU_BLOCKS = 8


def _dims(d_model):
    bw = d_model // 4
    dm = dict(
        bw=bw,
        da_sub=bw // (2 * DA_HEADS),
        da_head=bw // DA_HEADS,
        lru_bw=bw // LRU_BLOCKS,
        nope=128,
        rope=64,
        mla_v=bw // MLA_HEADS,
        q_lora=3 * d_model // 16,
        kv_lora=d_model // 8,
        gate_rank=d_model // 16,
    )
    dm["da_rot"] = dm["da_sub"] // 4
    dm["mla_qk"] = dm["nope"] + dm["rope"]
    assert dm["da_head"] == LANES and dm["mla_v"] == LANES and dm["lru_bw"] == LANES
    return dm


IN_TILE = 256


def _h_layout(dm):
    bw = dm["bw"]
    sizes = (("a_q", bw), ("a_k", bw), ("a_v", bw), ("b_gate", bw), ("b_x", bw), ("c_qa", dm["q_lora"]),
             ("c_kva", dm["kv_lora"]), ("c_kpe", dm["rope"]), ("g_lr", dm["gate_rank"]))
    src, pos = {}, 0
    for name, width in sizes:
        src[name] = (pos, width)
        pos += width
    main_cols = src["c_kpe"][0]
    assert main_cols % IN_TILE == 0 and all(src[n][0] % IN_TILE == 0 and src[n][1] % IN_TILE == 0 for n, _ in sizes[:-2])
    off = {name: src[name][0] for name, _ in sizes[:-2]}
    tail = dict(g_lr=0, c_kpe=dm["gate_rank"])
    tail_cols = dm["gate_rank"] + LANES
    return src, off, main_cols, tail, tail_cols


def _vmem(*nbytes):
    return min(int(sum(nbytes) * 1.25) + (2 << 20), VMEM_LIMIT_V7X)


def _nbytes(shape, dtype):
    return math.prod(shape) * jnp.dtype(dtype).itemsize


def _sigmoid(x):
    return 0.5 * jnp.tanh(0.5 * x) + 0.5


def _rms(x, g):
    return x * lax.rsqrt(jnp.mean(x * x, axis=-1, keepdims=True) + EPS) * g


NORM_ROWS = 256


def _norm_rows(x_ref, g_ref, xn_ref):
    @pl.when(pl.program_id(1) == 0)
    def _():
        g = g_ref[...]

        def body(r, carry):
            rows = pl.ds(pl.multiple_of(r * NORM_ROWS, NORM_ROWS), NORM_ROWS)
            xn_ref[rows, :] = _rms(x_ref[rows, :], g).astype(BF16)
            return carry

        lax.fori_loop(0, x_ref.shape[0] // NORM_ROWS, body, 0)


def _bf16(w):
    return w if w.dtype == BF16 else w.astype(BF16)


def _dot_nt(a, w_t):
    return lax.dot_general(a, _bf16(w_t), (((1,), (1,)), ((), ())), preferred_element_type=F32)


def _norm_matmul_kernel(x_ref, g_ref, w_ref, wt_ref, o_ref, ot_ref, xn_ref):
    _norm_rows(x_ref, g_ref, xn_ref)
    o_ref[...] = _dot_nt(xn_ref[...], w_ref[...])

    @pl.when(pl.program_id(1) == pl.num_programs(1) - 1)
    def _():
        ot_ref[...] = _dot_nt(xn_ref[...], wt_ref[...])


def _norm_matmul(x, g, w, w_tail, l, n, *, tm, tn):
    m, k = x.shape
    nt = w_tail.shape[1]
    return pl.pallas_call(
        _norm_matmul_kernel,
        grid=(m // tm, n // tn),
        in_specs=[
            pl.BlockSpec((tm, k), lambda i, j: (i, 0), pipeline_mode=pl.Buffered(1)),
            pl.BlockSpec((None, 1, k), lambda i, j: (l, 0, 0)),
            pl.BlockSpec((None, tn, k), lambda i, j: (l, j, 0)),
            pl.BlockSpec((None, nt, k), lambda i, j: (l, 0, 0)),
        ],
        out_specs=[pl.BlockSpec((tm, tn), lambda i, j: (i, j)), pl.BlockSpec((tm, nt), lambda i, j: (i, 0))],
        out_shape=[jax.ShapeDtypeStruct((m, n), F32), jax.ShapeDtypeStruct((m, nt), F32)],
        scratch_shapes=[pltpu.VMEM((tm, k), BF16)],
        compiler_params=pltpu.CompilerParams(
            dimension_semantics=("parallel", "arbitrary"),
            vmem_limit_bytes=_vmem(_nbytes((tm, k), F32), 2 * _nbytes((k, tn), w.dtype), 2 * _nbytes((k, nt), w_tail.dtype),
                                   2 * _nbytes((tm, tn + nt), F32), _nbytes((tm, k), BF16), 3 * _nbytes((NORM_ROWS, k), F32),
                                   2 * _nbytes((tm, nt), F32)),
        ),
        name="in_proj",
    )(x, g, w, w_tail)


def _ffn_up_kernel(x_ref, g_ref, wg_ref, wu_ref, o_ref, xn_ref):
    _norm_rows(x_ref, g_ref, xn_ref)
    xn = xn_ref[...]
    a = jnp.dot(xn, _bf16(wg_ref[...]), preferred_element_type=F32)
    b = jnp.dot(xn, _bf16(wu_ref[...]), preferred_element_type=F32)
    o_ref[...] = (a * _sigmoid(a) * b).astype(o_ref.dtype)


def _ffn_up(x, g, wg, wu, l, *, tm, tn):
    m, k = x.shape
    n = wg.shape[-1]
    return pl.pallas_call(
        _ffn_up_kernel,
        grid=(m // tm, n // tn),
        in_specs=[
            pl.BlockSpec((tm, k), lambda i, j: (i, 0), pipeline_mode=pl.Buffered(1)),
            pl.BlockSpec((None, 1, k), lambda i, j: (l, 0, 0)),
            pl.BlockSpec((None, k, tn), lambda i, j: (l, 0, j)),
            pl.BlockSpec((None, k, tn), lambda i, j: (l, 0, j)),
        ],
        out_specs=pl.BlockSpec((tm, tn), lambda i, j: (i, j)),
        out_shape=jax.ShapeDtypeStruct((m, n), BF16),
        scratch_shapes=[pltpu.VMEM((tm, k), BF16)],
        compiler_params=pltpu.CompilerParams(
            dimension_semantics=("parallel", "arbitrary"),
            vmem_limit_bytes=_vmem(_nbytes((tm, k), F32), 4 * _nbytes((k, tn), wg.dtype), 2 * _nbytes((k, tn), BF16),
                                   2 * _nbytes((tm, tn), BF16), _nbytes((tm, k), BF16), 3 * _nbytes((NORM_ROWS, k), F32),
                                   4 * _nbytes((tm, tn), F32)),
        ),
        name="ffn_up",
    )(x, g, wg, wu)


def _matmul_res_kernel(a_ref, w_ref, r_ref, o_ref):
    o_ref[...] = r_ref[...] + jnp.dot(a_ref[...], _bf16(w_ref[...]), preferred_element_type=F32)


def _matmul_res(a, w, l, res, *, tm, tn, name, in_place=True):
    m, k = a.shape
    n = w.shape[-1]
    return pl.pallas_call(
        _matmul_res_kernel,
        grid=(m // tm, n // tn),
        in_specs=[
            pl.BlockSpec((tm, k), lambda i, j: (i, 0)),
            pl.BlockSpec((None, k, tn), lambda i, j: (l, 0, j)),
            pl.BlockSpec((tm, tn), lambda i, j: (i, j)),
        ],
        out_specs=pl.BlockSpec((tm, tn), lambda i, j: (i, j)),
        out_shape=jax.ShapeDtypeStruct((m, n), F32),
        input_output_aliases={2: 0} if in_place else {},
        compiler_params=pltpu.CompilerParams(
            dimension_semantics=("parallel", "parallel"),
            vmem_limit_bytes=_vmem(2 * _nbytes((tm, k), BF16), 2 * _nbytes((k, tn), w.dtype), _nbytes((k, tn), BF16),
                                   5 * _nbytes((tm, tn), F32)),
        ),
        name=name,
    )(a, w, res)


def _branch_kernel(oa_ref, ob_ref, oc_ref, glr_ref, wbr_ref, wg0_ref, wg1_ref, wg2_ref, bg_ref, z_ref):
    glr = glr_ref[...].astype(BF16)
    acc = None
    for n, (o_ref, wg_ref) in enumerate(((oa_ref, wg0_ref), (ob_ref, wg1_ref), (oc_ref, wg2_ref))):
        y = jnp.dot(o_ref[...], _bf16(wbr_ref[n]), preferred_element_type=F32)
        gate = _sigmoid(jnp.dot(glr, _bf16(wg_ref[...]), preferred_element_type=F32) + bg_ref[n:n + 1, :])
        acc = gate * y if acc is None else acc + gate * y
    z_ref[...] = acc.astype(z_ref.dtype)


def _branch(o_a, o_b, o_c, h_tail, glr_off, w_br, w_gate, b_gate, l, *, tm, tn):
    m, bw = o_a.shape
    d = w_br.shape[-1]
    rank = w_gate.shape[1]
    nj = d // tn
    o_spec = pl.BlockSpec((tm, bw), lambda i, j: (i, 0))
    wg_specs = [pl.BlockSpec((None, rank, tn), functools.partial(lambda i, j, n: (l, 0, n * nj + j), n=n))
                for n in range(N_BRANCH)]
    return pl.pallas_call(
        _branch_kernel,
        grid=(m // tm, nj),
        in_specs=[
            o_spec, o_spec, o_spec,
            pl.BlockSpec((tm, rank), lambda i, j: (i, glr_off // rank)),
            pl.BlockSpec((None, N_BRANCH, bw, tn), lambda i, j: (l, 0, 0, j)),
            *wg_specs,
            pl.BlockSpec((None, N_BRANCH, tn), lambda i, j: (l, 0, j)),
        ],
        out_specs=pl.BlockSpec((tm, tn), lambda i, j: (i, j)),
        out_shape=jax.ShapeDtypeStruct((m, d), BF16),
        compiler_params=pltpu.CompilerParams(
            dimension_semantics=("parallel", "parallel"),
            vmem_limit_bytes=_vmem(6 * _nbytes((tm, bw), BF16), 2 * _nbytes((tm, rank), F32),
                                   2 * _nbytes((N_BRANCH, bw + rank, tn), w_br.dtype), _nbytes((bw + rank, tn), BF16),
                                   2 * _nbytes((tm, tn), BF16), 5 * _nbytes((tm, tn), F32)),
        ),
        name="branch",
    )(o_a, o_b, o_c, h_tail, w_br, w_gate, w_gate, w_gate, b_gate)


def _rope(y, tab, shift):
    c, s1, s2 = tab[:, 0:LANES], tab[:, LANES:2 * LANES], tab[:, 2 * LANES:3 * LANES]
    return y * c + pltpu.roll(y, LANES - shift, 1) * s1 + pltpu.roll(y, shift, 1) * s2


def _da_prep_kernel(q_ref, k_ref, v_ref, gq_ref, gk_ref, tab_ref, qo_ref, ko_ref, vo_ref, *, sub, rot, scale):
    tab = tab_ref[...]
    lo = lax.broadcasted_iota(jnp.int32, (q_ref.shape[0], LANES), 1) < sub

    def prep(x_ref, g_ref, o_ref, out_scale):
        for h in range(DA_HEADS):
            cols = slice(h * LANES, (h + 1) * LANES)
            x = x_ref[:, cols]
            sq = x * x
            ss0 = jnp.sum(jnp.where(lo, sq, 0.0), axis=-1, keepdims=True)
            ss1 = jnp.sum(jnp.where(lo, 0.0, sq), axis=-1, keepdims=True)
            y = x * lax.rsqrt(jnp.where(lo, ss0, ss1) * (1.0 / sub) + EPS) * g_ref[...]
            o_ref[:, cols] = (_rope(y, tab, rot // 2) * out_scale).astype(o_ref.dtype)

    prep(q_ref, gq_ref, qo_ref, scale)
    prep(k_ref, gk_ref, ko_ref, 1.0)
    _store_transposed(v_ref[...], vo_ref)


def _store_transposed(v, vt_ref):
    for h in range(v.shape[1] // LANES):
        cols = slice(h * LANES, (h + 1) * LANES)
        vt_ref[cols, :] = v[:, cols].T.astype(vt_ref.dtype)


def _vt_spec(tm, width, seq):
    per_seq = seq // tm
    return pl.BlockSpec((width, tm), lambda i: (i // per_seq, i % per_seq))


def _da_prep(h, off, gq, gk, tab, l, batch, seq, dm, *, tm):
    m = h.shape[0]
    bw = dm["bw"]
    col = lambda name: pl.BlockSpec((tm, bw), functools.partial(lambda i, c: (i, c), c=off[name] // bw))
    g_spec = pl.BlockSpec((None, 1, LANES), lambda i: (l, 0, 0))
    out = jax.ShapeDtypeStruct((m, bw), BF16)
    o_spec = pl.BlockSpec((tm, bw), lambda i: (i, 0))
    return pl.pallas_call(
        functools.partial(_da_prep_kernel, sub=dm["da_sub"], rot=dm["da_rot"], scale=dm["da_sub"] ** -0.5 * LOG2_E),
        grid=(m // tm,),
        in_specs=[col("a_q"), col("a_k"), col("a_v"), g_spec, g_spec,
                  pl.BlockSpec((tm, 3 * LANES), lambda i: (i, 0))],
        out_specs=[o_spec, o_spec, _vt_spec(tm, bw, seq)],
        out_shape=[out, out, jax.ShapeDtypeStruct((batch * bw, seq), BF16)],
        compiler_params=pltpu.CompilerParams(
            dimension_semantics=("parallel",),
            vmem_limit_bytes=_vmem(6 * _nbytes((tm, bw), F32), 6 * _nbytes((tm, bw), BF16), 2 * _nbytes((tm, 3 * LANES), F32)),
        ),
        name="da_prep",
    )(h, h, h, gq, gk, tab)


def _flash_t(q, k_ref, vt_ref, tk):
    rows, seq = q.shape[0], k_ref.shape[0]
    m = jnp.full((1, rows), -jnp.inf, F32)
    l = jnp.zeros((1, rows), F32)
    acc = jnp.zeros((vt_ref.shape[0], rows), F32)
    for c in range(seq // tk):
        kc = k_ref[c * tk:(c + 1) * tk, :]
        s = lax.dot_general(kc, q, (((1,), (1,)), ((), ())), preferred_element_type=F32)
        m_next = jnp.maximum(m, jnp.max(s, axis=0, keepdims=True))
        p = jnp.exp2(s - m_next)
        alpha = jnp.exp2(m - m_next)
        l = alpha * l + jnp.sum(p, axis=0, keepdims=True)
        acc = alpha * acc + jnp.dot(vt_ref[:, c * tk:(c + 1) * tk], p.astype(BF16), preferred_element_type=F32)
        m = m_next
    return acc / l


def _da_attn_kernel(lam_ref, q_ref, k_ref, vt_ref, g_ref, o_ref, *, tk, sub, lam_init):
    q = q_ref[...]
    tq = q.shape[0]
    lo = lax.broadcasted_iota(jnp.int32, q.shape, 1) < sub
    zero = jnp.zeros_like(q)
    qs = jnp.concatenate([jnp.where(lo, q, zero), jnp.where(lo, zero, q)], axis=0)
    ot = _flash_t(qs, k_ref, vt_ref, tk)
    lp = lam_ref[...]
    lam = (jnp.exp(jnp.sum(lp[0:1] * lp[1:2], axis=-1, keepdims=True))
           - jnp.exp(jnp.sum(lp[2:3] * lp[3:4], axis=-1, keepdims=True)) + lam_init)
    o = (ot[:, :tq] - lam * ot[:, tq:]).T
    o_ref[...] = (_rms(o, g_ref[...]) * (1.0 - lam_init)).astype(o_ref.dtype)


def _da_attn(q, k, vt, lam_p, g, l, batch, seq, dm, lam_init, *, tq, tk):
    m, bw = q.shape
    nq = seq // tq
    tk = min(tk, seq)
    assert seq % tq == 0 and seq % tk == 0
    return pl.pallas_call(
        functools.partial(_da_attn_kernel, tk=tk, sub=dm["da_sub"], lam_init=lam_init),
        grid=(batch, DA_HEADS, nq),
        in_specs=[
            pl.BlockSpec((None, 4, dm["da_sub"]), lambda b, h, i: (l, 0, 0)),
            pl.BlockSpec((tq, LANES), lambda b, h, i: (b * nq + i, h)),
            pl.BlockSpec((seq, LANES), lambda b, h, i: (b, h)),
            pl.BlockSpec((LANES, seq), lambda b, h, i: (b * DA_HEADS + h, 0)),
            pl.BlockSpec((None, 1, LANES), lambda b, h, i: (l, 0, 0)),
        ],
        out_specs=pl.BlockSpec((tq, LANES), lambda b, h, i: (b * nq + i, h)),
        out_shape=jax.ShapeDtypeStruct((m, bw), BF16),
        compiler_params=pltpu.CompilerParams(
            dimension_semantics=("parallel", "parallel", "arbitrary"),
            vmem_limit_bytes=_vmem(4 * _nbytes((seq, LANES), BF16), 4 * _nbytes((tq, LANES), BF16),
                                   2 * (seq // tk) * _nbytes((2 * tq, tk), F32)),
        ),
        name="da_attn",
    )(lam_p, q, k, vt, g)


def _mla_attn_kernel(q_ref, k_ref, vt_ref, o_ref, *, tk):
    o_ref[...] = _flash_t(q_ref[...], k_ref, vt_ref, tk).T.astype(o_ref.dtype)


def _mla_attn(q, k, vt, batch, seq, *, tq, tk):
    m = q.shape[0]
    dk = q.shape[1] // MLA_HEADS
    nq = seq // tq
    tk = min(tk, seq)
    assert seq % tq == 0 and seq % tk == 0
    return pl.pallas_call(
        functools.partial(_mla_attn_kernel, tk=tk),
        grid=(batch, MLA_HEADS, nq),
        in_specs=[
            pl.BlockSpec((tq, dk), lambda b, h, i: (b * nq + i, h)),
            pl.BlockSpec((seq, dk), lambda b, h, i: (b, h)),
            pl.BlockSpec((LANES, seq), lambda b, h, i: (b * MLA_HEADS + h, 0)),
        ],
        out_specs=pl.BlockSpec((tq, LANES), lambda b, h, i: (b * nq + i, h)),
        out_shape=jax.ShapeDtypeStruct((m, MLA_HEADS * LANES), BF16),
        compiler_params=pltpu.CompilerParams(
            dimension_semantics=("parallel", "parallel", "arbitrary"),
            vmem_limit_bytes=_vmem(4 * _nbytes((seq, dk + LANES), BF16), 4 * _nbytes((tq, dk), BF16),
                                   2 * (seq // tk) * _nbytes((tq, tk), F32)),
        ),
        name="mla_attn",
    )(q, k, vt)


def _col_blocks(tm, start, width):
    assert start % IN_TILE == 0 and width % IN_TILE == 0
    return [pl.BlockSpec((tm, IN_TILE), functools.partial(lambda i, c: (i, c), c=start // IN_TILE + t))
            for t in range(width // IN_TILE)]


def _mla_q_kernel(*refs, n_in, qk, rope, scale):
    cq_refs, (ga_ref, w_ref, gq_ref, tab_ref, o_ref) = refs[:n_in], refs[n_in:]
    xn = _rms(jnp.concatenate([r[...] for r in cq_refs], axis=1), ga_ref[...]).astype(BF16)
    q = jnp.dot(xn, w_ref[...], preferred_element_type=F32)
    tab = tab_ref[...]
    hw = 2 * LANES
    for h in range(MLA_HEADS):
        blk = q[:, h * hw:(h + 1) * hw]
        sc = lax.rsqrt(jnp.sum(blk * blk, axis=-1, keepdims=True) * (1.0 / qk) + EPS)
        y = blk * sc * gq_ref[:, h * hw:(h + 1) * hw]
        o_ref[:, h * hw:h * hw + LANES] = (_rope(y[:, :LANES], tab, rope // 2) * scale).astype(o_ref.dtype)
        o_ref[:, h * hw + LANES:(h + 1) * hw] = (y[:, LANES:] * scale).astype(o_ref.dtype)


def _mla_q(h, off, ga, w, gq, tab, l, dm, *, tm):
    m = h.shape[0]
    r = dm["q_lora"]
    n = w.shape[-1]
    return pl.pallas_call(
        functools.partial(_mla_q_kernel, n_in=r // IN_TILE, qk=dm["mla_qk"], rope=dm["rope"],
                          scale=dm["mla_qk"] ** -0.5 * LOG2_E),
        grid=(m // tm,),
        in_specs=[
            *_col_blocks(tm, off["c_qa"], r),
            pl.BlockSpec((None, 1, r), lambda i: (l, 0, 0)),
            pl.BlockSpec((None, r, n), lambda i: (l, 0, 0)),
            pl.BlockSpec((None, 1, n), lambda i: (l, 0, 0)),
            pl.BlockSpec((tm, 3 * LANES), lambda i: (i, 0)),
        ],
        out_specs=pl.BlockSpec((tm, n), lambda i: (i, 0)),
        out_shape=jax.ShapeDtypeStruct((m, n), BF16),
        compiler_params=pltpu.CompilerParams(
            dimension_semantics=("parallel",),
            vmem_limit_bytes=_vmem(2 * _nbytes((tm, r), F32), 2 * _nbytes((r, n), BF16), 2 * _nbytes((tm, n), BF16),
                                   3 * _nbytes((tm, n), F32)),
        ),
        name="mla_q",
    )(*[h] * (r // IN_TILE), ga, w, gq, tab)


def _mla_kv_kernel(*refs, n_in, qk, rope):
    ckv_refs, (kpe_ref, ga_ref, w_ref, gpe_ref, gn_ref, tab_ref, k_ref, v_ref) = refs[:n_in], refs[n_in:]
    xn = _rms(jnp.concatenate([r[...] for r in ckv_refs], axis=1), ga_ref[...]).astype(BF16)
    kv = jnp.dot(xn, w_ref[...], preferred_element_type=F32)
    nv = MLA_HEADS * LANES
    _store_transposed(kv[:, nv:], v_ref)
    kpe = kpe_ref[...]
    ss_pe = jnp.sum(kpe * kpe, axis=-1, keepdims=True)
    pe = _rope(kpe * gpe_ref[...], tab_ref[...], rope // 2)
    hw = 2 * LANES
    for h in range(MLA_HEADS):
        kn = kv[:, h * LANES:(h + 1) * LANES]
        sc = lax.rsqrt((ss_pe + jnp.sum(kn * kn, axis=-1, keepdims=True)) * (1.0 / qk) + EPS)
        k_ref[:, h * hw:h * hw + LANES] = (pe * sc).astype(k_ref.dtype)
        k_ref[:, h * hw + LANES:(h + 1) * hw] = (kn * gn_ref[:, h * LANES:(h + 1) * LANES] * sc).astype(k_ref.dtype)


def _mla_kv(h, off, h_tail, tail, ga, w, gpe, gn, tab, l, batch, seq, dm, *, tm):
    m = h.shape[0]
    r = dm["kv_lora"]
    n = w.shape[-1]
    nv = MLA_HEADS * LANES
    return pl.pallas_call(
        functools.partial(_mla_kv_kernel, n_in=r // IN_TILE, qk=dm["mla_qk"], rope=dm["rope"]),
        grid=(m // tm,),
        in_specs=[
            *_col_blocks(tm, off["c_kva"], r),
            pl.BlockSpec((tm, LANES), lambda i: (i, tail["c_kpe"] // LANES)),
            pl.BlockSpec((None, 1, r), lambda i: (l, 0, 0)),
            pl.BlockSpec((None, r, n), lambda i: (l, 0, 0)),
            pl.BlockSpec((None, 1, LANES), lambda i: (l, 0, 0)),
            pl.BlockSpec((None, 1, nv), lambda i: (l, 0, 0)),
            pl.BlockSpec((tm, 3 * LANES), lambda i: (i, 0)),
        ],
        out_specs=[pl.BlockSpec((tm, 2 * nv), lambda i: (i, 0)), _vt_spec(tm, nv, seq)],
        out_shape=[jax.ShapeDtypeStruct((m, 2 * nv), BF16), jax.ShapeDtypeStruct((batch * nv, seq), BF16)],
        compiler_params=pltpu.CompilerParams(
            dimension_semantics=("parallel",),
            vmem_limit_bytes=_vmem(2 * _nbytes((tm, r), F32), 2 * _nbytes((r, n), BF16), 2 * _nbytes((tm, 3 * nv), BF16),
                                   3 * _nbytes((tm, n), F32)),
        ),
        name="mla_kv",
    )(*[h] * (r // IN_TILE), h_tail, ga, w, gpe, gn, tab)


def _lru_kernel(u_ref, up_ref, un_ref, gt_ref, cw_ref, cb_ref, gw_ref, gb_ref, lam_ref, o_ref,
                ue_ref, a_ref, b_ref, h_ref, hb_ref, *, nt, ts):
    t = pl.program_id(1)
    backward = t < nt
    it = jnp.where(backward, nt - 1 - t, t - nt)
    width = u_ref.shape[1]
    halo = SUBLANES

    @pl.when((t == 0) | (t == nt))
    def _():
        h_ref[...] = jnp.zeros_like(h_ref)

    ue_ref[0:halo, :] = jnp.where(it > 0, up_ref[...], 0.0)
    ue_ref[halo:halo + ts, :] = u_ref[...]
    ue_ref[halo + ts:2 * halo + ts, :] = jnp.where(it < nt - 1, un_ref[...], 0.0)
    cw = cw_ref[...]
    pad_l = CONV_W // 2
    uc = cb_ref[...] + sum(cw[j:j + 1] * ue_ref[halo - pad_l + j:halo - pad_l + j + ts, :] for j in range(CONV_W))

    ucb = uc.astype(BF16)
    pre_r, pre_i = [], []
    for n in range(LRU_BLOCKS):
        g2 = jnp.dot(ucb[:, n * LANES:(n + 1) * LANES], gw_ref[n], preferred_element_type=F32)
        pre_r.append(g2[:, :LANES])
        pre_i.append(g2[:, LANES:])
    r = _sigmoid(jnp.concatenate(pre_r, axis=1) + gb_ref[0:1, :])
    i = _sigmoid(jnp.concatenate(pre_i, axis=1) + gb_ref[1:2, :])
    log_a = (-LRU_C * r) * jax.nn.softplus(-lam_ref[...])
    a_ref[...] = jnp.exp(log_a)
    th = jnp.tanh(log_a)
    b_ref[...] = jnp.sqrt(-2.0 * th / (1.0 - th)) * i * uc

    row = lax.broadcasted_iota(jnp.int32, (SUBLANES, width), 0)
    ng = ts // SUBLANES

    def scan_group(g, h, reverse):
        rows = pl.ds(pl.multiple_of(g * SUBLANES, SUBLANES), SUBLANES)
        a, b = a_ref[rows, :], b_ref[rows, :]
        for d in (1, 2, 4):
            shift = SUBLANES - d if reverse else d
            keep = (row < SUBLANES - d) if reverse else (row >= d)
            b = jnp.where(keep, a * pltpu.roll(b, shift, 0) + b, b)
            a = jnp.where(keep, a * pltpu.roll(a, shift, 0), a)
        return rows, a * h + b

    @pl.when(backward)
    def _():
        base = pl.multiple_of(it * ts, ts)

        def body(k, h):
            rows, hs = scan_group(ng - 1 - k, h, True)
            hb_ref[pl.ds(base + rows.start, SUBLANES), :] = hs
            return hs[0:1, :]

        h_ref[...] = lax.fori_loop(0, ng, body, h_ref[...])

    @pl.when(jnp.logical_not(backward))
    def _():
        def body(k, h):
            rows, hs = scan_group(k, h, False)
            b_ref[rows, :] = hs
            return hs[SUBLANES - 1:SUBLANES, :]

        h_ref[...] = lax.fori_loop(0, ng, body, h_ref[...])
        hsum = b_ref[...] + hb_ref[pl.ds(pl.multiple_of(it * ts, ts), ts), :]
        o_ref[...] = (hsum * jax.nn.gelu(gt_ref[...])).astype(o_ref.dtype)


def _lru(h, off, conv_w, conv_b, gate_w, gate_b, lam, l, batch, seq, dm, *, ts):
    m = h.shape[0]
    bw = dm["bw"]
    nt = seq // ts
    hb = ts // SUBLANES
    n_hblk = m // SUBLANES
    ucol, gcol = off["b_x"] // bw, off["b_gate"] // bw

    def tile(t):
        return jnp.where(t < nt, nt - 1 - t, t - nt)

    def direction(t):
        return jnp.where(t < nt, 1, 0)

    return pl.pallas_call(
        functools.partial(_lru_kernel, nt=nt, ts=ts),
        grid=(batch, 2 * nt),
        in_specs=[
            pl.BlockSpec((ts, bw), lambda b, t: (b * nt + tile(t), ucol)),
            pl.BlockSpec((SUBLANES, bw), lambda b, t: (jnp.maximum((b * nt + tile(t)) * hb - 1, 0), ucol)),
            pl.BlockSpec((SUBLANES, bw), lambda b, t: (jnp.minimum((b * nt + tile(t) + 1) * hb, n_hblk - 1), ucol)),
            pl.BlockSpec((ts, bw), lambda b, t: (b * nt + tile(t), gcol)),
            pl.BlockSpec((None, CONV_W, bw), lambda b, t: (l, 0, 0)),
            pl.BlockSpec((None, 1, bw), lambda b, t: (l, 0, 0)),
            pl.BlockSpec((None, None, LRU_BLOCKS, LANES, 2 * LANES), lambda b, t: (l, direction(t), 0, 0, 0)),
            pl.BlockSpec((None, None, 2, bw), lambda b, t: (l, direction(t), 0, 0)),
            pl.BlockSpec((None, None, 1, bw), lambda b, t: (l, direction(t), 0, 0)),
        ],
        out_specs=pl.BlockSpec((ts, bw), lambda b, t: (b * nt + jnp.maximum(t - nt, 0), 0)),
        out_shape=jax.ShapeDtypeStruct((m, bw), BF16),
        scratch_shapes=[
            pltpu.VMEM((ts + 2 * SUBLANES, bw), F32),
            pltpu.VMEM((ts, bw), F32),
            pltpu.VMEM((ts, bw), F32),
            pltpu.VMEM((1, bw), F32),
            pltpu.VMEM((seq, bw), F32),
        ],
        compiler_params=pltpu.CompilerParams(
            dimension_semantics=("parallel", "arbitrary"),
            vmem_limit_bytes=_vmem(_nbytes((seq, bw), F32), 16 * _nbytes((ts, bw), F32)),
        ),
        name="lru",
    )(h, h, h, h, conv_w, conv_b, gate_w, gate_b, lam)


def _rope_table(positions, rot_dim, period):
    half = rot_dim // 2
    inv = ROPE_THETA ** (-jnp.arange(0, rot_dim, 2, dtype=F32) / rot_dim)
    ang = positions.astype(F32)[..., None] * inv
    c, s = jnp.cos(ang), jnp.sin(ang)
    one = jnp.ones(ang.shape[:-1] + (period - rot_dim,), F32)
    zero = jnp.zeros(ang.shape[:-1] + (period - half,), F32)
    parts = [jnp.concatenate([c, c, one], -1),
             jnp.concatenate([-s, zero], -1),
             jnp.concatenate([zero[..., :half], s, zero[..., :period - rot_dim]], -1)]
    tab = jnp.concatenate([jnp.tile(p, LANES // period) for p in parts], -1)
    return tab.reshape(-1, 3 * LANES)


def kernel(x, positions, mixer_norm_g, w_in, da_q_norm_g, da_k_norm_g, da_lambda, da_subln_g, lru_conv_w, lru_conv_b, lru_gate_w, lru_gate_b, lru_L, mla_q_a_norm_g, mla_w_uq, mla_kv_a_norm_g, mla_w_ukv, mla_q_norm_g, mla_k_norm_g, w_gate_up, b_gate, w_branch_out, w_out, ffn_norm_g, w_ffn_gate, w_ffn_up, w_ffn_down):
    batch, seq, d = x.shape
    depth = w_in.shape[0]
    dm = _dims(d)
    src, off, main_cols, tail, tail_cols = _h_layout(dm)
    bw, rope, nope = dm["bw"], dm["rope"], dm["nope"]
    m = batch * seq

    w_in_t = jnp.swapaxes(w_in, 1, 2).astype(BF16)

    def w_in_rows(name):
        start, width = src[name]
        return w_in_t[:, start:start + width]

    w_tail = jnp.concatenate([w_in_rows("g_lr"), w_in_rows("c_kpe"),
                              jnp.zeros((depth, tail_cols - dm["gate_rank"] - rope, d), BF16)], 1)

    uq = mla_w_uq.reshape(depth, dm["q_lora"], MLA_HEADS, dm["mla_qk"])
    w_uq_p = jnp.concatenate([uq[..., :rope], jnp.zeros(uq.shape[:3] + (LANES - rope,), uq.dtype), uq[..., rope:]], -1)
    w_uq_p = w_uq_p.reshape(depth, dm["q_lora"], MLA_HEADS * 2 * LANES).astype(BF16)
    ukv = mla_w_ukv.reshape(depth, dm["kv_lora"], MLA_HEADS, 2, nope)
    w_ukv_p = ukv.transpose(0, 1, 3, 2, 4).reshape(depth, dm["kv_lora"], 2 * MLA_HEADS * nope).astype(BF16)
    pad = jnp.zeros((depth, LANES - rope), F32)
    gq_mla = jnp.tile(jnp.concatenate([mla_q_norm_g[:, :rope], pad, mla_q_norm_g[:, rope:]], -1), (1, MLA_HEADS))[:, None]
    gpe_mla = jnp.concatenate([mla_k_norm_g[:, :rope], pad], -1)[:, None]
    gn_mla = jnp.tile(mla_k_norm_g[:, rope:], (1, MLA_HEADS))[:, None]
    gq_da = jnp.tile(da_q_norm_g, (1, 2))[:, None]
    gk_da = jnp.tile(da_k_norm_g, (1, 2))[:, None]
    gate_w_p = lru_gate_w.transpose(0, 1, 3, 4, 2, 5).reshape(depth, 2, LRU_BLOCKS, LANES, 2 * LANES).astype(BF16)
    w_fd_p = w_ffn_down.astype(BF16)

    tab_da = _rope_table(positions, dm["da_rot"], dm["da_sub"])
    tab_mla = _rope_table(positions, rope, LANES)

    xf = x.reshape(m, d)
    for l in range(depth):
        lam_init = 0.8 - 0.6 * math.exp(-0.3 * l)
        h, h_tail = _norm_matmul(xf, mixer_norm_g[:, None], w_in_t, w_tail, l, main_cols, tm=1024, tn=IN_TILE)
        q_a, k_a, vt_a = _da_prep(h, off, gq_da, gk_da, tab_da, l, batch, seq, dm, tm=512)
        o_a = _da_attn(q_a, k_a, vt_a, da_lambda, da_subln_g[:, None], l, batch, seq, dm, lam_init, tq=512, tk=4096)
        o_b = _lru(h, off, lru_conv_w, lru_conv_b[:, None], gate_w_p, lru_gate_b, lru_L[:, :, None], l, batch, seq, dm,
                   ts=256)
        q_c = _mla_q(h, off, mla_q_a_norm_g[:, None], w_uq_p, gq_mla, tab_mla, l, dm, tm=512)
        k_c, vt_c = _mla_kv(h, off, h_tail, tail, mla_kv_a_norm_g[:, None], w_ukv_p, gpe_mla, gn_mla, tab_mla, l, batch,
                            seq, dm, tm=512)
        o_c = _mla_attn(q_c, k_c, vt_c, batch, seq, tq=1024, tk=4096)
        z = _branch(o_a, o_b, o_c, h_tail, tail["g_lr"], w_branch_out, w_gate_up, b_gate, l, tm=1024, tn=512)
        xf = _matmul_res(z, w_out, l, xf, tm=1024, tn=512, name="out_proj", in_place=l > 0)
        a = _ffn_up(xf, ffn_norm_g[:, None], w_ffn_gate, w_ffn_up, l, tm=1024, tn=256)
        xf = _matmul_res(a, w_fd_p, l, xf, tm=512, tn=512, name="ffn_down")
    return xf.reshape(batch, seq, d)
```
